```python
import math
import jax, jax.numpy as jnp
from jax import lax
import numpy as np

D_MODEL = 1024
BATCH = 2
SEQ = 16384
DEPTH = 4

N_MIXERS = 2
EXPAND = 2
D_INNER = EXPAND * D_MODEL
CONF_KERNEL = 31
HEADDIM = 64
SSD_HEADS = D_INNER // HEADDIM
SSD_GROUPS = 4
D_STATE = 128
SSD_CONV = 5
CHUNK = 128
SSD_CONV_DIM = D_INNER + 2 * SSD_GROUPS * D_STATE
SSD_IN = D_INNER + SSD_CONV_DIM + 2 * SSD_HEADS
N_CONF_LAYERS = (DEPTH + 1) // 2
N_SSD_LAYERS = DEPTH // 2
EPS = 1e-5

kernel_name = "bidir_conformer_ssd_hybrid"


def rmsnorm(x, w):
    xf = x.astype(jnp.float32)
    y = xf * lax.rsqrt(jnp.mean(xf * xf, axis=-1, keepdims=True) + EPS)
    return (y * w.astype(jnp.float32)).astype(x.dtype)


def layernorm(x, w, b):
    xf = x.astype(jnp.float32)
    mu = jnp.mean(xf, axis=-1, keepdims=True)
    xc = xf - mu
    var = jnp.mean(xc * xc, axis=-1, keepdims=True)
    y = xc * lax.rsqrt(var + EPS)
    return (y * w.astype(jnp.float32) + b.astype(jnp.float32)).astype(x.dtype)


def dwconv_centred(x, w, b):
    width, ch = w.shape
    pad = (width - 1) // 2
    y = lax.conv_general_dilated(
        x, w[:, None, :].astype(x.dtype), window_strides=(1,), padding=[(pad, pad)],
        dimension_numbers=("NWC", "WIO", "NWC"), feature_group_count=ch)
    return y + b.astype(x.dtype)


def conformer_mixer(h, w_in, dw_w, dw_b, ln_w, ln_b, w_out):
    proj = h @ w_in
    v, g, z = jnp.split(proj, 3, axis=-1)
    u = v * jax.nn.sigmoid(g)
    u = dwconv_centred(u, dw_w, dw_b)
    u = layernorm(u, ln_w, ln_b)
    u = jax.nn.silu(u) * jax.nn.silu(z)
    return u @ w_out


def ssd_chunked(x, dt, A, B, C):
    in_dtype = x.dtype
    f32 = jnp.float32
    x, dt, A, B, C = (t.astype(f32) for t in (x, dt, A, B, C))
    b, s, H, P = x.shape
    G, N = B.shape[2], B.shape[3]
    hpg = H // G
    nc = s // CHUNK
    xdt = (x * dt[..., None]).reshape(b, nc, CHUNK, G, hpg, P)
    a = (dt * A).reshape(b, nc, CHUNK, G, hpg)
    a = jnp.transpose(a, (0, 3, 4, 1, 2))
    Bc = B.reshape(b, nc, CHUNK, G, N)
    Cc = C.reshape(b, nc, CHUNK, G, N)
    acs = jnp.cumsum(a, axis=-1)

    diff = acs[..., :, None] - acs[..., None, :]
    mask = jnp.tril(jnp.ones((CHUNK, CHUNK), dtype=bool))
    Lmat = jnp.where(mask, jnp.exp(jnp.where(mask, diff, 0.0)), 0.0)
    CB = jnp.einsum("bclgn,bcsgn->bgcls", Cc, Bc)
    M = CB[:, :, None] * Lmat
    y_diag = jnp.einsum("bghcls,bcsghp->bclghp", M, xdt)

    decay_states = jnp.exp(acs[..., -1:] - acs)
    states = jnp.einsum("bclgn,bghcl,bclghp->bcghpn", Bc, decay_states, xdt)

    chunk_decay = jnp.exp(acs[..., -1])

    def step(hstate, inp):
        dec, st = inp
        return hstate * dec[..., None, None] + st, hstate

    h0 = jnp.zeros((b, G, hpg, P, N), f32)
    _, state_in = lax.scan(step, h0, (jnp.moveaxis(chunk_decay, 3, 0), jnp.moveaxis(states, 1, 0)))
    state_in = jnp.moveaxis(state_in, 0, 1)

    y_off = jnp.einsum("bclgn,bcghpn,bghcl->bclghp", Cc, state_in, jnp.exp(acs))
    y = (y_diag + y_off).reshape(b, s, H, P)
    return y.astype(in_dtype)


def ssd_mixer(h, w_in, conv_w, conv_b, dt_bias, A_log, D_skip, norm_w, w_out):
    b, s, _ = h.shape
    proj = h @ w_in
    z = proj[..., :D_INNER]
    xBC = proj[..., D_INNER:D_INNER + SSD_CONV_DIM]
    dt_raw = proj[..., D_INNER + SSD_CONV_DIM:]
    xBC = jax.nn.silu(dwconv_centred(xBC, conv_w, conv_b))
    xs = xBC[..., :D_INNER].reshape(b, s, SSD_HEADS, HEADDIM)
    Bm = xBC[..., D_INNER:D_INNER + SSD_GROUPS * D_STATE].reshape(b, s, SSD_GROUPS, D_STATE)
    Cm = xBC[..., D_INNER + SSD_GROUPS * D_STATE:].reshape(b, s, SSD_GROUPS, D_STATE)
    dt = jax.nn.softplus(dt_raw.reshape(b, s, 2, SSD_HEADS).astype(jnp.float32)
                         + dt_bias.astype(jnp.float32))
    A = -jnp.exp(A_log.astype(jnp.float32))
    y_fwd = ssd_chunked(xs, dt[:, :, 0], A[0], Bm, Cm)
    flip = lambda t: jnp.flip(t, axis=1)
    y_bwd = flip(ssd_chunked(flip(xs), flip(dt[:, :, 1]), A[1], flip(Bm), flip(Cm)))
    y = y_fwd + y_bwd + xs * D_skip[:, None].astype(xs.dtype)
    y = y.reshape(b, s, D_INNER)
    y = rmsnorm(y * jax.nn.silu(z), norm_w)
    return y @ w_out


def setup_inputs(seed: int = 0) -> dict:
    key = jax.random.key(seed)
    ks = jax.random.split(key, 20)
    f32 = jnp.float32
    NC, NS = N_CONF_LAYERS, N_SSD_LAYERS
    x = jax.random.normal(ks[0], (BATCH, SEQ, D_MODEL), f32)
    norm_w = 1.0 + 0.02 * jax.random.normal(ks[1], (DEPTH, D_MODEL), f32)
    final_norm_w = 1.0 + 0.02 * jax.random.normal(ks[2], (D_MODEL,), f32)
    cm_w_in = jax.random.normal(ks[3], (NC, D_MODEL, 3 * D_INNER), f32) * D_MODEL ** -0.5
    cm_dw_w = jax.random.normal(ks[4], (NC, CONF_KERNEL, D_INNER), f32) * CONF_KERNEL ** -0.5
    cm_dw_b = 0.02 * jax.random.normal(ks[5], (NC, D_INNER), f32)
    cm_ln_w = 1.0 + 0.02 * jax.random.normal(ks[6], (NC, D_INNER), f32)
    cm_ln_b = 0.02 * jax.random.normal(ks[7], (NC, D_INNER), f32)
    cm_w_out = jax.random.normal(ks[8], (NC, D_INNER, D_MODEL), f32) * D_INNER ** -0.5
    ssd_w_in = jax.random.normal(ks[9], (NS, D_MODEL, SSD_IN), f32) * D_MODEL ** -0.5
    ssd_conv_w = jax.random.normal(ks[10], (NS, SSD_CONV, SSD_CONV_DIM), f32) * SSD_CONV ** -0.5
    ssd_conv_b = 0.02 * jax.random.normal(ks[11], (NS, SSD_CONV_DIM), f32)
    u = jax.random.uniform(ks[12], (NS, 2, SSD_HEADS), f32)
    dt0 = jnp.exp(u * (math.log(0.1) - math.log(0.001)) + math.log(0.001))
    ssd_dt_bias = dt0 + jnp.log(-jnp.expm1(-dt0))
    ssd_A_log = jnp.log(jax.random.uniform(ks[13], (NS, 2, SSD_HEADS), f32, 1.0, 16.0))
    ssd_D = 1.0 + 0.1 * jax.random.normal(ks[14], (NS, SSD_HEADS), f32)
    ssd_norm_w = 1.0 + 0.02 * jax.random.normal(ks[15], (NS, D_INNER), f32)
    ssd_w_out = jax.random.normal(ks[16], (NS, D_INNER, D_MODEL), f32) * D_INNER ** -0.5
    return {"x": x, "norm_w": norm_w, "final_norm_w": final_norm_w,
            "cm_w_in": cm_w_in, "cm_dw_w": cm_dw_w, "cm_dw_b": cm_dw_b,
            "cm_ln_w": cm_ln_w, "cm_ln_b": cm_ln_b, "cm_w_out": cm_w_out,
            "ssd_w_in": ssd_w_in, "ssd_conv_w": ssd_conv_w, "ssd_conv_b": ssd_conv_b,
            "ssd_dt_bias": ssd_dt_bias, "ssd_A_log": ssd_A_log, "ssd_D": ssd_D,
            "ssd_norm_w": ssd_norm_w, "ssd_w_out": ssd_w_out}


def reference(x, norm_w, final_norm_w, cm_w_in, cm_dw_w, cm_dw_b, cm_ln_w, cm_ln_b, cm_w_out,
              ssd_w_in, ssd_conv_w, ssd_conv_b, ssd_dt_bias, ssd_A_log, ssd_D, ssd_norm_w, ssd_w_out):
    h = x
    for i in range(DEPTH):
        hn = rmsnorm(h, norm_w[i])
        j = i // N_MIXERS
        if i % N_MIXERS == 0:
            out = conformer_mixer(hn, cm_w_in[j], cm_dw_w[j], cm_dw_b[j], cm_ln_w[j], cm_ln_b[j], cm_w_out[j])
        else:
            out = ssd_mixer(hn, ssd_w_in[j], ssd_conv_w[j], ssd_conv_b[j], ssd_dt_bias[j], ssd_A_log[j],
                            ssd_D[j], ssd_norm_w[j], ssd_w_out[j])
        h = h + out
    return rmsnorm(h, final_norm_w)
```

```python
import functools

import jax
import jax.numpy as jnp
from jax import lax
from jax.experimental import pallas as pl
from jax.experimental.pallas import tpu as pltpu

F32 = jnp.float32
BF16 = jnp.bfloat16

EPS = 1e-5
D_MODEL = 1024
D_INNER = 2048
CONF_KERNEL = 31
HEADDIM = 64
SSD_HEADS = 32
SSD_GROUPS = 4
D_STATE = 128
SSD_CONV = 5
CHUNK = 128
SSD_CONV_DIM = D_INNER + 2 * SSD_GROUPS * D_STATE
GROUP_WIDTH = D_INNER // SSD_GROUPS
LANES = 128
HALO = 16
CONV_ROWS = 32
VMEM_LIMIT = 56 * 1024 * 1024


def _sigmoid(x):
    return 1.0 / (1.0 + jnp.exp(-x))


def _silu(x):
    return x * _sigmoid(x)


def _softplus(x):
    return jnp.maximum(x, 0.0) + jnp.log1p(jnp.exp(-jnp.abs(x)))


def _rms_bf16(x, w):
    ms = jnp.mean(x * x, axis=-1, keepdims=True)
    return (x * lax.rsqrt(ms + EPS) * w).astype(BF16)


def _dot(a, b):
    return jnp.dot(a, b, preferred_element_type=F32)


def _split3(a):
    a1 = a.astype(BF16)
    r1 = a - a1.astype(F32)
    a2 = r1.astype(BF16)
    a3 = (r1 - a2.astype(F32)).astype(BF16)
    return a1, a2, a3


def _normed_tile(hp_ref, hm_ref, hx_ref, nw_ref, ts):
    i = pl.program_id(1)
    nt = pl.num_programs(1)
    hall = jnp.concatenate([hp_ref[0], hm_ref[0], hx_ref[0]], axis=0)
    hn_all = _rms_bf16(hall, nw_ref[...])
    rows = lax.broadcasted_iota(jnp.int32, (ts + 2 * HALO, 1), 0)
    lo = jnp.where(i == 0, HALO, 0)
    hi = jnp.where(i == nt - 1, HALO + ts, ts + 2 * HALO)
    valid = (rows >= lo) & (rows < hi)
    return hn_all, valid


def _dwconv_rows(u_scr, w, b, width, ts, store):
    pad = (width - 1) // 2
    lc = w.shape[1]
    for rb in range(ts // CONV_ROWS):
        r0 = rb * CONV_ROWS
        acc = jnp.broadcast_to(b, (CONV_ROWS, lc))
        for k in range(width):
            s = r0 + HALO - pad + k
            acc = acc + w[k:k + 1, :] * u_scr[s:s + CONV_ROWS, :]
        store(r0, acc)


def _conformer_kernel(hp_ref, hm_ref, hx_ref, nw_ref, win_ref, dww_ref, dwb_ref, lnw_ref, lnb_ref,
                      wout_ref, o_ref, u_scr, c_scr, *, ts, lc):
    hn_all, valid = _normed_tile(hp_ref, hm_ref, hx_ref, nw_ref, ts)

    def chunk(c, carry):
        off_v = pl.multiple_of(c * lc, lc)
        off_g = pl.multiple_of(D_INNER + c * lc, lc)
        v = _dot(hn_all, win_ref[:, pl.ds(off_v, lc)])
        g = _dot(hn_all, win_ref[:, pl.ds(off_g, lc)])
        u_scr[...] = jnp.where(valid, v * _sigmoid(g), 0.0)
        w = dww_ref[:, pl.ds(off_v, lc)]
        b = dwb_ref[:, pl.ds(off_v, lc)]

        def store(r0, acc):
            c_scr[r0:r0 + CONV_ROWS, pl.ds(off_v, lc)] = acc

        _dwconv_rows(u_scr, w, b, CONF_KERNEL, ts, store)
        return carry

    lax.fori_loop(0, D_INNER // lc, chunk, 0)

    cv = c_scr[...]
    mu = jnp.mean(cv, axis=-1, keepdims=True)
    xc = cv - mu
    var = jnp.mean(xc * xc, axis=-1, keepdims=True)
    y = xc * lax.rsqrt(var + EPS) * lnw_ref[...] + lnb_ref[...]
    z = _dot(hn_all[HALO:HALO + ts], win_ref[:, 2 * D_INNER:3 * D_INNER])
    gate = (_silu(y) * _silu(z)).astype(BF16)
    o_ref[0] = hm_ref[0] + _dot(gate, wout_ref[...])


def _tile_specs(ts, d):
    per = ts // HALO

    def prev_map(b, i):
        return (b, jnp.maximum(i * per - 1, 0), 0)

    def make_next_map(n_halo_blocks):
        def next_map(b, i):
            return (b, jnp.minimum((i + 1) * per, n_halo_blocks - 1), 0)
        return next_map

    return prev_map, make_next_map


def _resident(shape):
    nd = len(shape)
    return pl.BlockSpec(shape, lambda b, i: (0,) * nd, pipeline_mode=pl.Buffered(1))


def _conformer_layer(h, nw, w_in, dw_w, dw_b, ln_w, ln_b, w_out, *, ts=256, lc=256):
    bsz, s, d = h.shape
    assert s % ts == 0 and ts % CONV_ROWS == 0 and ts % HALO == 0
    prev_map, make_next_map = _tile_specs(ts, d)
    next_map = make_next_map(s // HALO)
    e = D_INNER
    kern = functools.partial(_conformer_kernel, ts=ts, lc=lc)
    return pl.pallas_call(
        kern,
        grid=(bsz, s // ts),
        in_specs=[
            pl.BlockSpec((1, HALO, d), prev_map),
            pl.BlockSpec((1, ts, d), lambda b, i: (b, i, 0)),
            pl.BlockSpec((1, HALO, d), next_map),
            _resident((1, d)),
            _resident((d, 3 * e)),
            _resident((CONF_KERNEL, e)),
            _resident((1, e)),
            _resident((1, e)),
            _resident((1, e)),
            _resident((e, d)),
        ],
        out_specs=pl.BlockSpec((1, ts, d), lambda b, i: (b, i, 0)),
        out_shape=jax.ShapeDtypeStruct((bsz, s, d), F32),
        scratch_shapes=[
            pltpu.VMEM((ts + 2 * HALO, lc), F32),
            pltpu.VMEM((ts, e), F32),
        ],
        compiler_params=pltpu.CompilerParams(
            dimension_semantics=("arbitrary", "arbitrary"), vmem_limit_bytes=VMEM_LIMIT),
        name="conformer_layer",
    )(h, h, h, nw.reshape(1, d), w_in.astype(BF16), dw_w, dw_b.reshape(1, e), ln_w.reshape(1, e),
      ln_b.reshape(1, e), w_out.astype(BF16))


def _ssd_in_kernel(hp_ref, hm_ref, hx_ref, nw_ref, wzx_ref, wdt_ref, cw_ref, cb_ref, dtb_ref,
                   z_ref, xbc_ref, dt_ref, u_scr, *, ts, lc):
    hn_all, valid = _normed_tile(hp_ref, hm_ref, hx_ref, nw_ref, ts)
    hn_m = hn_all[HALO:HALO + ts]
    z_ref[0] = _dot(hn_m, wzx_ref[:, 0:D_INNER]).astype(BF16)
    dt_ref[0] = _softplus(_dot(hn_m, wdt_ref[...]) + dtb_ref[...])

    def chunk(c, carry):
        off = pl.multiple_of(c * lc, lc)
        off_w = pl.multiple_of(D_INNER + c * lc, lc)
        p = _dot(hn_all, wzx_ref[:, pl.ds(off_w, lc)])
        u_scr[...] = jnp.where(valid, p, 0.0)
        w = cw_ref[:, pl.ds(off, lc)]
        b = cb_ref[:, pl.ds(off, lc)]

        def store(r0, acc):
            xbc_ref[0, r0:r0 + CONV_ROWS, pl.ds(off, lc)] = _silu(acc).astype(BF16)

        _dwconv_rows(u_scr, w, b, SSD_CONV, ts, store)
        return carry

    lax.fori_loop(0, SSD_CONV_DIM // lc, chunk, 0)


def _ssd_in_layer(h, nw, w_zx, w_dt, conv_w, conv_b, dt_bias, *, ts=256, lc=256):
    bsz, s, d = h.shape
    assert s % ts == 0 and ts % CONV_ROWS == 0 and ts % HALO == 0
    prev_map, make_next_map = _tile_specs(ts, d)
    next_map = make_next_map(s // HALO)
    e, cd = D_INNER, SSD_CONV_DIM
    kern = functools.partial(_ssd_in_kernel, ts=ts, lc=lc)
    tile = lambda b, i: (b, i, 0)
    return pl.pallas_call(
        kern,
        grid=(bsz, s // ts),
        in_specs=[
            pl.BlockSpec((1, HALO, d), prev_map),
            pl.BlockSpec((1, ts, d), tile),
            pl.BlockSpec((1, HALO, d), next_map),
            _resident((1, d)),
            _resident((d, e + cd)),
            _resident((d, 2 * LANES)),
            _resident((SSD_CONV, cd)),
            _resident((1, cd)),
            _resident((1, 2 * LANES)),
        ],
        out_specs=[
            pl.BlockSpec((1, ts, e), tile),
            pl.BlockSpec((1, ts, cd), tile),
            pl.BlockSpec((1, ts, 2 * LANES), tile),
        ],
        out_shape=[
            jax.ShapeDtypeStruct((bsz, s, e), BF16),
            jax.ShapeDtypeStruct((bsz, s, cd), BF16),
            jax.ShapeDtypeStruct((bsz, s, 2 * LANES), F32),
        ],
        scratch_shapes=[pltpu.VMEM((ts + 2 * HALO, lc), F32)],
        compiler_params=pltpu.CompilerParams(
            dimension_semantics=("arbitrary", "arbitrary"), vmem_limit_bytes=VMEM_LIMIT),
        name="ssd_in_proj",
    )(h, h, h, nw.reshape(1, d), w_zx, w_dt, conv_w, conv_b.reshape(1, cd), dt_bias)


def _ssd_scan_kernel(x_ref, b_ref, c_ref, dt_ref, alog_ref, e3_ref, y_ref, state_scr, *, rev):
    @pl.when(pl.program_id(1) == 0)
    def _():
        state_scr[...] = jnp.zeros_like(state_scr)

    L = CHUNK
    x = x_ref[0]
    bm = b_ref[0]
    cm = c_ref[0]
    dt = dt_ref[0]
    a = dt * (-jnp.exp(alog_ref[...]))

    r = lax.broadcasted_iota(jnp.int32, (L, L), 0)
    cdx = lax.broadcasted_iota(jnp.int32, (L, L), 1)
    tri = (cdx >= r) if rev else (cdx <= r)
    tri_t = (r >= cdx) if rev else (r <= cdx)
    tri_b = jnp.where(tri, 1.0, 0.0).astype(BF16)
    tri_tb = jnp.where(tri_t, 1.0, 0.0).astype(BF16)

    a1, a2, a3 = _split3(a)
    acs = _dot(tri_b, a1) + _dot(tri_b, a2) + _dot(tri_b, a3)
    t1, t2, t3 = _split3(a.T)
    acs_t = _dot(t1, tri_tb) + _dot(t2, tri_tb) + _dot(t3, tri_tb)
    dt_t = dt.T

    last = acs[0:1, :] if rev else acs[L - 1:L, :]
    w3 = jnp.exp(acs)
    w2 = dt * jnp.exp(last - acs)

    def expand(wn):
        p1, p2, p3 = _split3(wn)
        return _dot(jnp.concatenate([p1, p2, p3], axis=1), e3_ref[...])

    w3e = expand(w3)
    w2e = expand(w2)
    dec_row = w3e[0:1, :] if rev else w3e[L - 1:L, :]

    lane = lax.broadcasted_iota(jnp.int32, (L, LANES), 1)
    lo_half = lane < HEADDIM

    for g in range(SSD_GROUPS):
        bg = bm[:, g * D_STATE:(g + 1) * D_STATE]
        cg = cm[:, g * D_STATE:(g + 1) * D_STATE]
        gsl = slice(g * GROUP_WIDTH, (g + 1) * GROUP_WIDTH)
        cb = lax.dot_general(cg, bg, (((1,), (1,)), ((), ())), preferred_element_type=F32)
        yds = []
        for qq in range(GROUP_WIDTH // LANES):
            q = g * (GROUP_WIDTH // LANES) + qq
            ms = []
            for hh in (2 * q, 2 * q + 1):
                d = acs[:, hh:hh + 1] - acs_t[hh:hh + 1, :]
                lm = jnp.where(tri, jnp.exp(jnp.where(tri, d, 0.0)), 0.0)
                ms.append((cb * lm * dt_t[hh:hh + 1, :]).astype(BF16))
            mcat = jnp.concatenate(ms, axis=1)
            xq = x[:, q * LANES:(q + 1) * LANES]
            zero = jnp.zeros_like(xq)
            xbd = jnp.concatenate([jnp.where(lo_half, xq, zero), jnp.where(lo_half, zero, xq)], axis=0)
            yds.append(_dot(mcat, xbd))
        st = state_scr[:, gsl]
        y_off = _dot(cg, st.astype(BF16)) * w3e[:, gsl]
        y_ref[0, :, gsl] = jnp.concatenate(yds, axis=1) + y_off
        xs = (x[:, gsl].astype(F32) * w2e[:, gsl]).astype(BF16)
        bg_t = bg.astype(F32).T.astype(BF16)
        state_scr[:, gsl] = st * dec_row[:, gsl] + _dot(bg_t, xs)


def _ssd_scan(xbc, dt, alog_pad, e3, *, rev):
    bsz, s, _ = xbc.shape
    nc = s // CHUNK
    e = D_INNER
    nb = D_INNER // (SSD_GROUPS * D_STATE)
    d_idx = 1 if rev else 0

    def cidx(j):
        return (nc - 1 - j) if rev else j

    kern = functools.partial(_ssd_scan_kernel, rev=rev)
    gw = SSD_GROUPS * D_STATE
    return pl.pallas_call(
        kern,
        grid=(bsz, nc),
        in_specs=[
            pl.BlockSpec((1, CHUNK, e), lambda b, j: (b, cidx(j), 0)),
            pl.BlockSpec((1, CHUNK, gw), lambda b, j: (b, cidx(j), nb)),
            pl.BlockSpec((1, CHUNK, gw), lambda b, j: (b, cidx(j), nb + 1)),
            pl.BlockSpec((1, CHUNK, LANES), lambda b, j: (b, cidx(j), d_idx)),
            pl.BlockSpec((1, LANES), lambda b, j: (0, d_idx)),
            pl.BlockSpec((3 * LANES, e), lambda b, j: (0, 0)),
        ],
        out_specs=pl.BlockSpec((1, CHUNK, e), lambda b, j: (b, cidx(j), 0)),
        out_shape=jax.ShapeDtypeStruct((bsz, s, e), F32),
        scratch_shapes=[pltpu.VMEM((D_STATE, e), F32)],
        compiler_params=pltpu.CompilerParams(
            dimension_semantics=("arbitrary", "arbitrary"), vmem_limit_bytes=VMEM_LIMIT),
        name="ssd_scan_bwd" if rev else "ssd_scan_fwd",
    )(xbc, xbc, xbc, dt, alog_pad, e3)


def _ssd_out_kernel(h_ref, yf_ref, yb_ref, x_ref, z_ref, dexp_ref, gnw_ref, wout_ref, fnw_ref, o_ref,
                    *, final):
    y = yf_ref[0] + yb_ref[0] + x_ref[0].astype(F32) * dexp_ref[...]
    gated = y * _silu(z_ref[0].astype(F32))
    out = h_ref[0] + _dot(_rms_bf16(gated, gnw_ref[...]), wout_ref[...])
    if final:
        ms = jnp.mean(out * out, axis=-1, keepdims=True)
        out = out * lax.rsqrt(ms + EPS) * fnw_ref[...]
    o_ref[0] = out


def _ssd_out_layer(h, yf, yb, xbc, z, dexp, gnw, w_out, fnw, *, final, ts=256):
    bsz, s, d = h.shape
    e = D_INNER
    tile = lambda b, i: (b, i, 0)
    kern = functools.partial(_ssd_out_kernel, final=final)
    return pl.pallas_call(
        kern,
        grid=(bsz, s // ts),
        in_specs=[
            pl.BlockSpec((1, ts, d), tile),
            pl.BlockSpec((1, ts, e), tile),
            pl.BlockSpec((1, ts, e), tile),
            pl.BlockSpec((1, ts, e), tile),
            pl.BlockSpec((1, ts, e), tile),
            _resident((1, e)),
            _resident((1, e)),
            _resident((e, d)),
            _resident((1, d)),
        ],
        out_specs=pl.BlockSpec((1, ts, d), tile),
        out_shape=jax.ShapeDtypeStruct((bsz, s, d), F32),
        compiler_params=pltpu.CompilerParams(
            dimension_semantics=("arbitrary", "arbitrary"), vmem_limit_bytes=VMEM_LIMIT),
        name="ssd_out_proj",
    )(h, yf, yb, xbc, z, dexp, gnw.reshape(1, e), w_out.astype(BF16), fnw.reshape(1, d))


def _pad_lanes(v):
    out = jnp.zeros((2, LANES), v.dtype).at[:, :SSD_HEADS].set(v)
    return out.reshape(1, 2 * LANES)


def _expand_matrix():
    r = jnp.arange(3 * LANES)[:, None] % LANES
    c = jnp.arange(D_INNER)[None, :] // HEADDIM
    return (r == c).astype(BF16)


def _ssd_layer(h, nw, w_in, conv_w, conv_b, dt_bias, a_log, d_skip, gnw, w_out, fnw, *, final):
    d = h.shape[-1]
    e, cd = D_INNER, SSD_CONV_DIM
    w_zx = w_in[:, :e + cd].astype(BF16)
    w_dt_raw = w_in[:, e + cd:]
    w_dt = jnp.zeros((d, 2, LANES), F32).at[:, :, :SSD_HEADS].set(
        w_dt_raw.reshape(d, 2, SSD_HEADS)).reshape(d, 2 * LANES).astype(BF16)
    z, xbc, dt = _ssd_in_layer(h, nw, w_zx, w_dt, conv_w, conv_b, _pad_lanes(dt_bias))
    alog_pad = _pad_lanes(a_log)
    e3 = _expand_matrix()
    yf = _ssd_scan(xbc, dt, alog_pad, e3, rev=False)
    yb = _ssd_scan(xbc, dt, alog_pad, e3, rev=True)
    dexp = jnp.repeat(d_skip, HEADDIM).reshape(1, e)
    return _ssd_out_layer(h, yf, yb, xbc, z, dexp, gnw, w_out, fnw, final=final)


def kernel(x, norm_w, final_norm_w, cm_w_in, cm_dw_w, cm_dw_b, cm_ln_w, cm_ln_b, cm_w_out, ssd_w_in,
           ssd_conv_w, ssd_conv_b, ssd_dt_bias, ssd_A_log, ssd_D, ssd_norm_w, ssd_w_out):
    depth = norm_w.shape[0]
    assert depth % 2 == 0
    h = x
    for i in range(depth):
        j = i // 2
        if i % 2 == 0:
            h = _conformer_layer(h, norm_w[i], cm_w_in[j], cm_dw_w[j], cm_dw_b[j], cm_ln_w[j],
                                 cm_ln_b[j], cm_w_out[j])
        else:
            h = _ssd_layer(h, norm_w[i], ssd_w_in[j], ssd_conv_w[j], ssd_conv_b[j], ssd_dt_bias[j],
                           ssd_A_log[j], ssd_D[j], ssd_norm_w[j], ssd_w_out[j], final_norm_w,
                           final=(i == depth - 1))
    return h
```

```python
import functools

import jax
import jax.numpy as jnp
from jax import lax
from jax.experimental import pallas as pl
from jax.experimental.pallas import tpu as pltpu

F32 = jnp.float32
BF16 = jnp.bfloat16

EPS = 1e-5
D_MODEL = 1024
D_INNER = 2048
CONF_KERNEL = 31
HEADDIM = 64
SSD_HEADS = 32
SSD_GROUPS = 4
D_STATE = 128
SSD_CONV = 5
CHUNK = 128
SSD_CONV_DIM = D_INNER + 2 * SSD_GROUPS * D_STATE
GROUP_WIDTH = D_INNER // SSD_GROUPS
LANES = 128
HALO = 16
CONV_ROWS = 32
VMEM_LIMIT = 56 * 1024 * 1024


def _sigmoid(x):
    return 1.0 / (1.0 + jnp.exp(-x))


def _silu(x):
    return x * _sigmoid(x)


def _softplus(x):
    return jnp.maximum(x, 0.0) + jnp.log1p(jnp.exp(-jnp.abs(x)))


def _rms_bf16(x, w):
    ms = jnp.mean(x * x, axis=-1, keepdims=True)
    return (x * lax.rsqrt(ms + EPS) * w).astype(BF16)


def _dot(a, b):
    return jnp.dot(a, b, preferred_element_type=F32)


def _split3(a):
    a1 = a.astype(BF16)
    r1 = a - a1.astype(F32)
    a2 = r1.astype(BF16)
    a3 = (r1 - a2.astype(F32)).astype(BF16)
    return a1, a2, a3


def _normed_tile(hp_ref, hm_ref, hx_ref, nw_ref, ts):
    i = pl.program_id(1)
    nt = pl.num_programs(1)
    hall = jnp.concatenate([hp_ref[0], hm_ref[0], hx_ref[0]], axis=0)
    hn_all = _rms_bf16(hall, nw_ref[...])
    rows = lax.broadcasted_iota(jnp.int32, (ts + 2 * HALO, 1), 0)
    lo = jnp.where(i == 0, HALO, 0)
    hi = jnp.where(i == nt - 1, HALO + ts, ts + 2 * HALO)
    valid = (rows >= lo) & (rows < hi)
    return hn_all, valid


def _dwconv_rows(u_scr, w, b, width, ts, store):
    pad = (width - 1) // 2
    lc = w.shape[1]
    for rb in range(ts // CONV_ROWS):
        r0 = rb * CONV_ROWS
        acc = jnp.broadcast_to(b, (CONV_ROWS, lc))
        for k in range(width):
            s = r0 + HALO - pad + k
            acc = acc + w[k:k + 1, :] * u_scr[s:s + CONV_ROWS, :]
        store(r0, acc)


STREAMS = 16
PITCH_PAD = 8
SLABS = D_MODEL // LANES
CONF_PAD = (CONF_KERNEL - 1) // 2


def _conf_in_kernel(h_ref, nw_ref, win_ref, u_ref, z_ref, slab_scr, hnp_scr, *, p_rows, lc):
    pitch = p_rows + PITCH_PAD
    nw = nw_ref[...]
    for j in range(STREAMS):
        hj = h_ref[0, j]
        hn = hj * lax.rsqrt(jnp.mean(hj * hj, axis=-1, keepdims=True) + EPS) * nw
        for s in range(SLABS):
            slab_scr[s, j * pitch:j * pitch + p_rows, :] = hn[:, s * LANES:(s + 1) * LANES]

    def permute_in(r, carry):
        row0 = pl.multiple_of(r * STREAMS, STREAMS)
        for s in range(SLABS):
            piece = slab_scr[s, pl.ds(r, STREAMS, stride=pitch), :]
            hnp_scr[pl.ds(row0, STREAMS), s * LANES:(s + 1) * LANES] = piece.astype(BF16)
        return carry

    lax.fori_loop(0, p_rows, permute_in, 0)

    def chunk(c, carry):
        off_v = pl.multiple_of(c * lc, lc)
        off_g = pl.multiple_of(D_INNER + c * lc, lc)
        off_z = pl.multiple_of(2 * D_INNER + c * lc, lc)
        lhs = hnp_scr[...]
        u = _dot(lhs, win_ref[:, pl.ds(off_v, lc)]) * _sigmoid(_dot(lhs, win_ref[:, pl.ds(off_g, lc)]))
        for q in range(lc // LANES):
            u_ref[0, c * (lc // LANES) + q] = u[:, q * LANES:(q + 1) * LANES].reshape(
                p_rows, STREAMS, LANES).astype(BF16)
        z_ref[0, :, pl.ds(off_v, lc)] = _dot(lhs, win_ref[:, pl.ds(off_z, lc)]).astype(BF16)
        return carry

    lax.fori_loop(0, D_INNER // lc, chunk, 0)


def _conf_out_kernel(up_ref, um_ref, ux_ref, wrap_ref, z_ref, h_ref, w16_ref, dwb_ref, lnw_ref, lnb_ref,
                     wout_ref, o_ref, c_scr, res_scr, slab_scr, *, p_rows):
    lc = LANES
    pitch = p_rows + PITCH_PAD
    i = pl.program_id(1)
    nt = pl.num_programs(1)

    def u_row(c, idx):
        if idx < 0:
            return up_ref[0, c, HALO + idx]
        if idx >= p_rows:
            return ux_ref[0, c, idx - p_rows]
        return um_ref[0, c, idx]

    def chunk(c, carry):
        off = pl.multiple_of(c * lc, lc)
        lanes = pl.ds(off, lc)
        b = dwb_ref[:, lanes]
        for r in range(p_rows):
            acc = jnp.zeros((STREAMS, lc), F32)
            for k in range(CONF_KERNEL):
                acc = acc + u_row(c, r + k - CONF_PAD).astype(F32) * w16_ref[c, k].astype(F32)
            c_scr[r * STREAMS:(r + 1) * STREAMS, lanes] = acc + b

        def fix_edge(rows, true_row):
            for r in rows:
                acc = jnp.broadcast_to(b, (STREAMS, lc))
                for k in range(CONF_KERNEL):
                    acc = acc + true_row(r + k - CONF_PAD) * w16_ref[c, k].astype(F32)
                c_scr[r * STREAMS:(r + 1) * STREAMS, lanes] = acc

        zero_row = jnp.zeros((1, lc), F32)

        @pl.when(i == 0)
        def _():
            def true_row(idx):
                if idx >= 0:
                    return um_ref[0, c, idx].astype(F32)
                prev_stream = wrap_ref[0, c, HALO + idx].astype(F32)
                return jnp.concatenate([zero_row, prev_stream[0:STREAMS - 1]], axis=0)
            fix_edge(range(CONF_PAD), true_row)

        @pl.when(i == nt - 1)
        def _():
            def true_row(idx):
                if idx < p_rows:
                    return um_ref[0, c, idx].astype(F32)
                next_stream = wrap_ref[0, c, idx - p_rows].astype(F32)
                return jnp.concatenate([next_stream[1:STREAMS], zero_row], axis=0)
            fix_edge(range(p_rows - CONF_PAD, p_rows), true_row)

        return carry

    lax.fori_loop(0, D_INNER // lc, chunk, 0)

    cv = c_scr[...]
    mu = jnp.mean(cv, axis=-1, keepdims=True)
    xc = cv - mu
    var = jnp.mean(xc * xc, axis=-1, keepdims=True)
    y = xc * lax.rsqrt(var + EPS) * lnw_ref[...] + lnb_ref[...]
    gate = (_silu(y) * _silu(z_ref[0].astype(F32))).astype(BF16)
    res_scr[...] = _dot(gate, wout_ref[...])

    def permute_out(r, carry):
        row0 = pl.multiple_of(r * STREAMS, STREAMS)
        for s in range(SLABS):
            slab_scr[s, pl.ds(r, STREAMS, stride=pitch), :] = res_scr[pl.ds(row0, STREAMS),
                                                                     s * LANES:(s + 1) * LANES]
        return carry

    lax.fori_loop(0, p_rows, permute_out, 0)
    for j in range(STREAMS):
        for s in range(SLABS):
            lanes = slice(s * LANES, (s + 1) * LANES)
            o_ref[0, j, :, lanes] = h_ref[0, j, :, lanes] + slab_scr[s, j * pitch:j * pitch + p_rows, :]


def _resident(shape):
    nd = len(shape)
    return pl.BlockSpec(shape, lambda b, i: (0,) * nd, pipeline_mode=pl.Buffered(1))


def _conformer_layer(h, nw, w_in, dw_w, dw_b, ln_w, ln_b, w_out, *, p_rows=32, lc_in=512):
    bsz, s, d = h.shape
    e = D_INNER
    sl = s // STREAMS
    assert s % STREAMS == 0 and sl % p_rows == 0 and p_rows % HALO == 0 and p_rows >= CONF_PAD
    nt = sl // p_rows
    ts = p_rows * STREAMS
    nch = e // LANES
    h4 = h.reshape(bsz, STREAMS, sl, d)
    params = pltpu.CompilerParams(dimension_semantics=("arbitrary", "arbitrary"), vmem_limit_bytes=VMEM_LIMIT)
    slab = pltpu.VMEM((SLABS, STREAMS * (p_rows + PITCH_PAD), LANES), F32)
    h_spec = pl.BlockSpec((1, STREAMS, p_rows, d), lambda b, i: (b, 0, i, 0))

    u, z = pl.pallas_call(
        functools.partial(_conf_in_kernel, p_rows=p_rows, lc=lc_in),
        grid=(bsz, nt),
        in_specs=[h_spec, _resident((1, d)), _resident((d, 3 * e))],
        out_specs=[pl.BlockSpec((1, nch, p_rows, STREAMS, LANES), lambda b, i: (b, 0, i, 0, 0)),
                   pl.BlockSpec((1, ts, e), lambda b, i: (b, i, 0))],
        out_shape=[jax.ShapeDtypeStruct((bsz, nch, sl, STREAMS, LANES), BF16),
                   jax.ShapeDtypeStruct((bsz, sl * STREAMS, e), BF16)],
        scratch_shapes=[slab, pltpu.VMEM((ts, d), BF16)],
        compiler_params=params,
        name="conformer_in",
    )(h4, nw.reshape(1, d), w_in.astype(BF16))

    per = p_rows // HALO
    last_halo = sl // HALO - 1
    w16 = jnp.broadcast_to(dw_w.astype(BF16).reshape(CONF_KERNEL, 1, nch, LANES),
                           (CONF_KERNEL, STREAMS, nch, LANES)).transpose(2, 0, 1, 3)
    halo_shape = (1, nch, HALO, STREAMS, LANES)
    out4 = pl.pallas_call(
        functools.partial(_conf_out_kernel, p_rows=p_rows),
        grid=(bsz, nt),
        in_specs=[
            pl.BlockSpec(halo_shape, lambda b, i: (b, 0, jnp.maximum(i * per - 1, 0), 0, 0)),
            pl.BlockSpec((1, nch, p_rows, STREAMS, LANES), lambda b, i: (b, 0, i, 0, 0)),
            pl.BlockSpec(halo_shape, lambda b, i: (b, 0, jnp.minimum((i + 1) * per, last_halo), 0, 0)),
            pl.BlockSpec(halo_shape, lambda b, i: (b, 0, jnp.where(i == nt - 1, 0, last_halo), 0, 0)),
            pl.BlockSpec((1, ts, e), lambda b, i: (b, i, 0)),
            h_spec,
            _resident((nch, CONF_KERNEL, STREAMS, LANES)),
            _resident((1, e)),
            _resident((1, e)),
            _resident((1, e)),
            _resident((e, d)),
        ],
        out_specs=h_spec,
        out_shape=jax.ShapeDtypeStruct((bsz, STREAMS, sl, d), F32),
        scratch_shapes=[pltpu.VMEM((ts, e), F32), pltpu.VMEM((ts, d), F32), slab],
        compiler_params=params,
        name="conformer_out",
    )(u, u, u, u, z, h4, w16, dw_b.reshape(1, e), ln_w.reshape(1, e), ln_b.reshape(1, e), w_out.astype(BF16))
    return out4.reshape(bsz, s, d)


def _halo_maps(ts, s):
    per = ts // HALO
    last = s // HALO - 1

    def prev_map(b, i):
        return (b, jnp.maximum(i * per - 1, 0), 0)

    def next_map(b, i):
        return (b, jnp.minimum((i + 1) * per, last), 0)

    return prev_map, next_map


def _ssd_in_kernel(hp_ref, hm_ref, hx_ref, nw_ref, wzx_ref, wdt_ref, cw_ref, cb_ref, dtb_ref,
                   z_ref, xbc_ref, dt_ref, u_scr, *, ts, lc):
    hn_all, valid = _normed_tile(hp_ref, hm_ref, hx_ref, nw_ref, ts)
    hn_m = hn_all[HALO:HALO + ts]
    z_ref[0] = _dot(hn_m, wzx_ref[:, 0:D_INNER]).astype(BF16)
    dt_ref[0] = _softplus(_dot(hn_m, wdt_ref[...]) + dtb_ref[...])

    def chunk(c, carry):
        off = pl.multiple_of(c * lc, lc)
        off_w = pl.multiple_of(D_INNER + c * lc, lc)
        p = _dot(hn_all, wzx_ref[:, pl.ds(off_w, lc)])
        u_scr[...] = jnp.where(valid, p, 0.0)
        w = cw_ref[:, pl.ds(off, lc)]
        b = cb_ref[:, pl.ds(off, lc)]

        def store(r0, acc):
            xbc_ref[0, r0:r0 + CONV_ROWS, pl.ds(off, lc)] = _silu(acc).astype(BF16)

        _dwconv_rows(u_scr, w, b, SSD_CONV, ts, store)
        return carry

    lax.fori_loop(0, SSD_CONV_DIM // lc, chunk, 0)


def _ssd_in_layer(h, nw, w_zx, w_dt, conv_w, conv_b, dt_bias, *, ts=512, lc=512):
    bsz, s, d = h.shape
    assert s % ts == 0 and ts % CONV_ROWS == 0 and ts % HALO == 0
    prev_map, next_map = _halo_maps(ts, s)
    e, cd = D_INNER, SSD_CONV_DIM
    kern = functools.partial(_ssd_in_kernel, ts=ts, lc=lc)
    tile = lambda b, i: (b, i, 0)
    return pl.pallas_call(
        kern,
        grid=(bsz, s // ts),
        in_specs=[
            pl.BlockSpec((1, HALO, d), prev_map),
            pl.BlockSpec((1, ts, d), tile),
            pl.BlockSpec((1, HALO, d), next_map),
            _resident((1, d)),
            _resident((d, e + cd)),
            _resident((d, 2 * LANES)),
            _resident((SSD_CONV, cd)),
            _resident((1, cd)),
            _resident((1, 2 * LANES)),
        ],
        out_specs=[
            pl.BlockSpec((1, ts, e), tile),
            pl.BlockSpec((1, ts, cd), tile),
            pl.BlockSpec((1, ts, 2 * LANES), tile),
        ],
        out_shape=[
            jax.ShapeDtypeStruct((bsz, s, e), BF16),
            jax.ShapeDtypeStruct((bsz, s, cd), BF16),
            jax.ShapeDtypeStruct((bsz, s, 2 * LANES), F32),
        ],
        scratch_shapes=[pltpu.VMEM((ts + 2 * HALO, lc), F32)],
        compiler_params=pltpu.CompilerParams(
            dimension_semantics=("arbitrary", "arbitrary"), vmem_limit_bytes=VMEM_LIMIT),
        name="ssd_in_proj",
    )(h, h, h, nw.reshape(1, d), w_zx, w_dt, conv_w, conv_b.reshape(1, cd), dt_bias)


def _ssd_scan_kernel(x_ref, b_ref, c_ref, dt_ref, alog_ref, e3_ref, y_ref, state_scr, *, rev):
    @pl.when(pl.program_id(1) == 0)
    def _():
        state_scr[...] = jnp.zeros_like(state_scr)

    L = CHUNK
    x = x_ref[0]
    bm = b_ref[0]
    cm = c_ref[0]
    dt = dt_ref[0]
    a = dt * (-jnp.exp(alog_ref[...]))

    r = lax.broadcasted_iota(jnp.int32, (L, L), 0)
    cdx = lax.broadcasted_iota(jnp.int32, (L, L), 1)
    tri = (cdx >= r) if rev else (cdx <= r)
    tri_t = (r >= cdx) if rev else (r <= cdx)
    tri_b = jnp.where(tri, 1.0, 0.0).astype(BF16)
    tri_tb = jnp.where(tri_t, 1.0, 0.0).astype(BF16)

    a1, a2, a3 = _split3(a)
    acs = _dot(tri_b, a1) + _dot(tri_b, a2) + _dot(tri_b, a3)
    t1, t2, t3 = _split3(a.T)
    acs_t = _dot(t1, tri_tb) + _dot(t2, tri_tb) + _dot(t3, tri_tb)
    dt_t = dt.T

    last = acs[0:1, :] if rev else acs[L - 1:L, :]
    w3 = jnp.exp(acs)
    w2 = dt * jnp.exp(last - acs)

    def expand(wn):
        p1, p2, p3 = _split3(wn)
        return _dot(jnp.concatenate([p1, p2, p3], axis=1), e3_ref[...])

    w3e = expand(w3)
    w2e = expand(w2)
    dec_row = w3e[0:1, :] if rev else w3e[L - 1:L, :]

    lane = lax.broadcasted_iota(jnp.int32, (L, LANES), 1)
    lo_half = lane < HEADDIM

    for g in range(SSD_GROUPS):
        bg = bm[:, g * D_STATE:(g + 1) * D_STATE]
        cg = cm[:, g * D_STATE:(g + 1) * D_STATE]
        gsl = slice(g * GROUP_WIDTH, (g + 1) * GROUP_WIDTH)
        cb = lax.dot_general(cg, bg, (((1,), (1,)), ((), ())), preferred_element_type=F32)
        yds = []
        for qq in range(GROUP_WIDTH // LANES):
            q = g * (GROUP_WIDTH // LANES) + qq
            ms = []
            for hh in (2 * q, 2 * q + 1):
                d = acs[:, hh:hh + 1] - acs_t[hh:hh + 1, :]
                lm = jnp.where(tri, jnp.exp(jnp.where(tri, d, 0.0)), 0.0)
                ms.append((cb * lm * dt_t[hh:hh + 1, :]).astype(BF16))
            mcat = jnp.concatenate(ms, axis=1)
            xq = x[:, q * LANES:(q + 1) * LANES]
            zero = jnp.zeros_like(xq)
            xbd = jnp.concatenate([jnp.where(lo_half, xq, zero), jnp.where(lo_half, zero, xq)], axis=0)
            yds.append(_dot(mcat, xbd))
        st = state_scr[:, gsl]
        y_off = _dot(cg, st.astype(BF16)) * w3e[:, gsl]
        y_ref[0, :, gsl] = jnp.concatenate(yds, axis=1) + y_off
        xs = (x[:, gsl].astype(F32) * w2e[:, gsl]).astype(BF16)
        bg_t = bg.astype(F32).T.astype(BF16)
        state_scr[:, gsl] = st * dec_row[:, gsl] + _dot(bg_t, xs)


def _ssd_scan(xbc, dt, alog_pad, e3, *, rev):
    bsz, s, _ = xbc.shape
    nc = s // CHUNK
    e = D_INNER
    nb = D_INNER // (SSD_GROUPS * D_STATE)
    d_idx = 1 if rev else 0

    def cidx(j):
        return (nc - 1 - j) if rev else j

    kern = functools.partial(_ssd_scan_kernel, rev=rev)
    gw = SSD_GROUPS * D_STATE
    return pl.pallas_call(
        kern,
        grid=(bsz, nc),
        in_specs=[
            pl.BlockSpec((1, CHUNK, e), lambda b, j: (b, cidx(j), 0)),
            pl.BlockSpec((1, CHUNK, gw), lambda b, j: (b, cidx(j), nb)),
            pl.BlockSpec((1, CHUNK, gw), lambda b, j: (b, cidx(j), nb + 1)),
            pl.BlockSpec((1, CHUNK, LANES), lambda b, j: (b, cidx(j), d_idx)),
            pl.BlockSpec((1, LANES), lambda b, j: (0, d_idx)),
            pl.BlockSpec((3 * LANES, e), lambda b, j: (0, 0)),
        ],
        out_specs=pl.BlockSpec((1, CHUNK, e), lambda b, j: (b, cidx(j), 0)),
        out_shape=jax.ShapeDtypeStruct((bsz, s, e), F32),
        scratch_shapes=[pltpu.VMEM((D_STATE, e), F32)],
        compiler_params=pltpu.CompilerParams(
            dimension_semantics=("arbitrary", "arbitrary"), vmem_limit_bytes=VMEM_LIMIT),
        name="ssd_scan_bwd" if rev else "ssd_scan_fwd",
    )(xbc, xbc, xbc, dt, alog_pad, e3)


def _ssd_out_kernel(h_ref, yf_ref, yb_ref, x_ref, z_ref, dexp_ref, gnw_ref, wout_ref, fnw_ref, o_ref,
                    *, final):
    y = yf_ref[0] + yb_ref[0] + x_ref[0].astype(F32) * dexp_ref[...]
    gated = y * _silu(z_ref[0].astype(F32))
    out = h_ref[0] + _dot(_rms_bf16(gated, gnw_ref[...]), wout_ref[...])
    if final:
        ms = jnp.mean(out * out, axis=-1, keepdims=True)
        out = out * lax.rsqrt(ms + EPS) * fnw_ref[...]
    o_ref[0] = out


def _ssd_out_layer(h, yf, yb, xbc, z, dexp, gnw, w_out, fnw, *, final, ts=256):
    bsz, s, d = h.shape
    e = D_INNER
    tile = lambda b, i: (b, i, 0)
    kern = functools.partial(_ssd_out_kernel, final=final)
    return pl.pallas_call(
        kern,
        grid=(bsz, s // ts),
        in_specs=[
            pl.BlockSpec((1, ts, d), tile),
            pl.BlockSpec((1, ts, e), tile),
            pl.BlockSpec((1, ts, e), tile),
            pl.BlockSpec((1, ts, e), tile),
            pl.BlockSpec((1, ts, e), tile),
            _resident((1, e)),
            _resident((1, e)),
            _resident((e, d)),
            _resident((1, d)),
        ],
        out_specs=pl.BlockSpec((1, ts, d), tile),
        out_shape=jax.ShapeDtypeStruct((bsz, s, d), F32),
        compiler_params=pltpu.CompilerParams(
            dimension_semantics=("arbitrary", "arbitrary"), vmem_limit_bytes=VMEM_LIMIT),
        name="ssd_out_proj",
    )(h, yf, yb, xbc, z, dexp, gnw.reshape(1, e), w_out.astype(BF16), fnw.reshape(1, d))


def _pad_lanes(v):
    out = jnp.zeros((2, LANES), v.dtype).at[:, :SSD_HEADS].set(v)
    return out.reshape(1, 2 * LANES)


def _expand_matrix():
    r = jnp.arange(3 * LANES)[:, None] % LANES
    c = jnp.arange(D_INNER)[None, :] // HEADDIM
    return (r == c).astype(BF16)


def _ssd_layer(h, nw, w_in, conv_w, conv_b, dt_bias, a_log, d_skip, gnw, w_out, fnw, *, final):
    d = h.shape[-1]
    e, cd = D_INNER, SSD_CONV_DIM
    w_zx = w_in[:, :e + cd].astype(BF16)
    w_dt_raw = w_in[:, e + cd:]
    w_dt = jnp.zeros((d, 2, LANES), F32).at[:, :, :SSD_HEADS].set(
        w_dt_raw.reshape(d, 2, SSD_HEADS)).reshape(d, 2 * LANES).astype(BF16)
    z, xbc, dt = _ssd_in_layer(h, nw, w_zx, w_dt, conv_w, conv_b, _pad_lanes(dt_bias))
    alog_pad = _pad_lanes(a_log)
    e3 = _expand_matrix()
    yf = _ssd_scan(xbc, dt, alog_pad, e3, rev=False)
    yb = _ssd_scan(xbc, dt, alog_pad, e3, rev=True)
    dexp = jnp.repeat(d_skip, HEADDIM).reshape(1, e)
    return _ssd_out_layer(h, yf, yb, xbc, z, dexp, gnw, w_out, fnw, final=final)


def kernel(x, norm_w, final_norm_w, cm_w_in, cm_dw_w, cm_dw_b, cm_ln_w, cm_ln_b, cm_w_out, ssd_w_in,
           ssd_conv_w, ssd_conv_b, ssd_dt_bias, ssd_A_log, ssd_D, ssd_norm_w, ssd_w_out):
    depth = norm_w.shape[0]
    assert depth % 2 == 0
    h = x
    for i in range(depth):
        j = i // 2
        if i % 2 == 0:
            h = _conformer_layer(h, norm_w[i], cm_w_in[j], cm_dw_w[j], cm_dw_b[j], cm_ln_w[j],
                                 cm_ln_b[j], cm_w_out[j])
        else:
            h = _ssd_layer(h, norm_w[i], ssd_w_in[j], ssd_conv_w[j], ssd_conv_b[j], ssd_dt_bias[j],
                           ssd_A_log[j], ssd_D[j], ssd_norm_w[j], ssd_w_out[j], final_norm_w,
                           final=(i == depth - 1))
    return h
```

```python
import functools

import jax
import jax.numpy as jnp
from jax import lax
from jax.experimental import pallas as pl
from jax.experimental.pallas import tpu as pltpu

F32 = jnp.float32
BF16 = jnp.bfloat16

EPS = 1e-5
D_MODEL = 1024
D_INNER = 2048
CONF_KERNEL = 31
HEADDIM = 64
SSD_HEADS = 32
SSD_GROUPS = 4
D_STATE = 128
SSD_CONV = 5
CHUNK = 128
SSD_CONV_DIM = D_INNER + 2 * SSD_GROUPS * D_STATE
GROUP_WIDTH = D_INNER // SSD_GROUPS
LANES = 128
HALO = 16
CONV_ROWS = 32
VMEM_LIMIT = 56 * 1024 * 1024


def _sigmoid(x):
    return 1.0 / (1.0 + jnp.exp(-x))


def _silu(x):
    return x * _sigmoid(x)


def _softplus(x):
    return jnp.maximum(x, 0.0) + jnp.log1p(jnp.exp(-jnp.abs(x)))


def _rms_bf16(x, w):
    ms = jnp.mean(x * x, axis=-1, keepdims=True)
    return (x * lax.rsqrt(ms + EPS) * w).astype(BF16)


def _dot(a, b):
    return jnp.dot(a, b, preferred_element_type=F32)


def _split3(a):
    a1 = a.astype(BF16)
    r1 = a - a1.astype(F32)
    a2 = r1.astype(BF16)
    a3 = (r1 - a2.astype(F32)).astype(BF16)
    return a1, a2, a3


def _normed_tile(hp_ref, hm_ref, hx_ref, nw_ref, ts):
    i = pl.program_id(1)
    nt = pl.num_programs(1)
    hall = jnp.concatenate([hp_ref[0], hm_ref[0], hx_ref[0]], axis=0)
    hn_all = _rms_bf16(hall, nw_ref[...])
    rows = lax.broadcasted_iota(jnp.int32, (ts + 2 * HALO, 1), 0)
    lo = jnp.where(i == 0, HALO, 0)
    hi = jnp.where(i == nt - 1, HALO + ts, ts + 2 * HALO)
    valid = (rows >= lo) & (rows < hi)
    return hn_all, valid


def _dwconv_slabs(u_ref, w, b, width, ts, store):
    pad = (width - 1) // 2
    for rb in range(ts // CONV_ROWS):
        r0 = rb * CONV_ROWS
        accs = []
        for s in range(u_ref.shape[0]):
            lanes = slice(s * LANES, (s + 1) * LANES)
            acc = jnp.broadcast_to(b[:, lanes], (CONV_ROWS, LANES))
            for k in range(width):
                acc = acc + w[k:k + 1, lanes] * u_ref[s, pl.ds(r0 + HALO - pad + k, CONV_ROWS), :]
            accs.append(acc)
        store(r0, jnp.concatenate(accs, axis=1))


STREAMS = 16
PITCH_PAD = 8
SLABS = D_MODEL // LANES
CONF_PAD = (CONF_KERNEL - 1) // 2


def _conf_in_kernel(h_ref, nw_ref, win_ref, u_ref, z_ref, slab_scr, hnp_scr, *, p_rows, lc):
    pitch = p_rows + PITCH_PAD
    nw = nw_ref[...]
    for j in range(STREAMS):
        hj = h_ref[0, j]
        hn = hj * lax.rsqrt(jnp.mean(hj * hj, axis=-1, keepdims=True) + EPS) * nw
        for s in range(SLABS):
            slab_scr[s, j * pitch:j * pitch + p_rows, :] = hn[:, s * LANES:(s + 1) * LANES]

    def permute_in(r, carry):
        row0 = pl.multiple_of(r * STREAMS, STREAMS)
        for s in range(SLABS):
            piece = slab_scr[s, pl.ds(r, STREAMS, stride=pitch), :]
            hnp_scr[pl.ds(row0, STREAMS), s * LANES:(s + 1) * LANES] = piece.astype(BF16)
        return carry

    lax.fori_loop(0, p_rows, permute_in, 0)

    def chunk(c, carry):
        off_v = pl.multiple_of(c * lc, lc)
        off_g = pl.multiple_of(D_INNER + c * lc, lc)
        off_z = pl.multiple_of(2 * D_INNER + c * lc, lc)
        lhs = hnp_scr[...]
        u = _dot(lhs, win_ref[:, pl.ds(off_v, lc)]) * _sigmoid(_dot(lhs, win_ref[:, pl.ds(off_g, lc)]))
        for q in range(lc // LANES):
            u_ref[0, c * (lc // LANES) + q] = u[:, q * LANES:(q + 1) * LANES].reshape(
                p_rows, STREAMS, LANES).astype(BF16)
        z_ref[0, :, pl.ds(off_v, lc)] = _dot(lhs, win_ref[:, pl.ds(off_z, lc)]).astype(BF16)
        return carry

    lax.fori_loop(0, D_INNER // lc, chunk, 0)


def _conf_out_kernel(up_ref, um_ref, ux_ref, wrap_ref, z_ref, h_ref, w16_ref, dwb_ref, lnw_ref, lnb_ref,
                     wout_ref, o_ref, c_scr, res_scr, slab_scr, *, p_rows):
    lc = LANES
    pitch = p_rows + PITCH_PAD
    i = pl.program_id(1)
    nt = pl.num_programs(1)

    def u_row(c, idx):
        if idx < 0:
            return up_ref[0, c, HALO + idx]
        if idx >= p_rows:
            return ux_ref[0, c, idx - p_rows]
        return um_ref[0, c, idx]

    def chunk(c, carry):
        off = pl.multiple_of(c * lc, lc)
        lanes = pl.ds(off, lc)
        b = dwb_ref[:, lanes]
        for r in range(p_rows):
            acc = jnp.zeros((STREAMS, lc), F32)
            for k in range(CONF_KERNEL):
                acc = acc + u_row(c, r + k - CONF_PAD).astype(F32) * w16_ref[c, k].astype(F32)
            c_scr[r * STREAMS:(r + 1) * STREAMS, lanes] = acc + b

        def fix_edge(rows, true_row):
            for r in rows:
                acc = jnp.broadcast_to(b, (STREAMS, lc))
                for k in range(CONF_KERNEL):
                    acc = acc + true_row(r + k - CONF_PAD) * w16_ref[c, k].astype(F32)
                c_scr[r * STREAMS:(r + 1) * STREAMS, lanes] = acc

        zero_row = jnp.zeros((1, lc), F32)

        @pl.when(i == 0)
        def _():
            def true_row(idx):
                if idx >= 0:
                    return um_ref[0, c, idx].astype(F32)
                prev_stream = wrap_ref[0, c, HALO + idx].astype(F32)
                return jnp.concatenate([zero_row, prev_stream[0:STREAMS - 1]], axis=0)
            fix_edge(range(CONF_PAD), true_row)

        @pl.when(i == nt - 1)
        def _():
            def true_row(idx):
                if idx < p_rows:
                    return um_ref[0, c, idx].astype(F32)
                next_stream = wrap_ref[0, c, idx - p_rows].astype(F32)
                return jnp.concatenate([next_stream[1:STREAMS], zero_row], axis=0)
            fix_edge(range(p_rows - CONF_PAD, p_rows), true_row)

        return carry

    lax.fori_loop(0, D_INNER // lc, chunk, 0)

    cv = c_scr[...]
    mu = jnp.mean(cv, axis=-1, keepdims=True)
    xc = cv - mu
    var = jnp.mean(xc * xc, axis=-1, keepdims=True)
    y = xc * lax.rsqrt(var + EPS) * lnw_ref[...] + lnb_ref[...]
    gate = (_silu(y) * _silu(z_ref[0].astype(F32))).astype(BF16)
    res_scr[...] = _dot(gate, wout_ref[...])

    def permute_out(r, carry):
        row0 = pl.multiple_of(r * STREAMS, STREAMS)
        for s in range(SLABS):
            slab_scr[s, pl.ds(r, STREAMS, stride=pitch), :] = res_scr[pl.ds(row0, STREAMS),
                                                                     s * LANES:(s + 1) * LANES]
        return carry

    lax.fori_loop(0, p_rows, permute_out, 0)
    for j in range(STREAMS):
        for s in range(SLABS):
            lanes = slice(s * LANES, (s + 1) * LANES)
            o_ref[0, j, :, lanes] = h_ref[0, j, :, lanes] + slab_scr[s, j * pitch:j * pitch + p_rows, :]


def _resident(shape):
    nd = len(shape)
    return pl.BlockSpec(shape, lambda b, i: (0,) * nd, pipeline_mode=pl.Buffered(1))


def _conformer_layer(h, nw, w_in, dw_w, dw_b, ln_w, ln_b, w_out, *, p_rows=32, lc_in=512):
    bsz, s, d = h.shape
    e = D_INNER
    sl = s // STREAMS
    assert s % STREAMS == 0 and sl % p_rows == 0 and p_rows % HALO == 0 and p_rows >= CONF_PAD
    nt = sl // p_rows
    ts = p_rows * STREAMS
    nch = e // LANES
    h4 = h.reshape(bsz, STREAMS, sl, d)
    params = pltpu.CompilerParams(dimension_semantics=("arbitrary", "arbitrary"), vmem_limit_bytes=VMEM_LIMIT)
    slab = pltpu.VMEM((SLABS, STREAMS * (p_rows + PITCH_PAD), LANES), F32)
    h_spec = pl.BlockSpec((1, STREAMS, p_rows, d), lambda b, i: (b, 0, i, 0))

    u, z = pl.pallas_call(
        functools.partial(_conf_in_kernel, p_rows=p_rows, lc=lc_in),
        grid=(bsz, nt),
        in_specs=[h_spec, _resident((1, d)), _resident((d, 3 * e))],
        out_specs=[pl.BlockSpec((1, nch, p_rows, STREAMS, LANES), lambda b, i: (b, 0, i, 0, 0)),
                   pl.BlockSpec((1, ts, e), lambda b, i: (b, i, 0))],
        out_shape=[jax.ShapeDtypeStruct((bsz, nch, sl, STREAMS, LANES), BF16),
                   jax.ShapeDtypeStruct((bsz, sl * STREAMS, e), BF16)],
        scratch_shapes=[slab, pltpu.VMEM((ts, d), BF16)],
        compiler_params=params,
        name="conformer_in",
    )(h4, nw.reshape(1, d), w_in.astype(BF16))

    per = p_rows // HALO
    last_halo = sl // HALO - 1
    w16 = jnp.broadcast_to(dw_w.astype(BF16).reshape(CONF_KERNEL, 1, nch, LANES),
                           (CONF_KERNEL, STREAMS, nch, LANES)).transpose(2, 0, 1, 3)
    halo_shape = (1, nch, HALO, STREAMS, LANES)
    out4 = pl.pallas_call(
        functools.partial(_conf_out_kernel, p_rows=p_rows),
        grid=(bsz, nt),
        in_specs=[
            pl.BlockSpec(halo_shape, lambda b, i: (b, 0, jnp.maximum(i * per - 1, 0), 0, 0)),
            pl.BlockSpec((1, nch, p_rows, STREAMS, LANES), lambda b, i: (b, 0, i, 0, 0)),
            pl.BlockSpec(halo_shape, lambda b, i: (b, 0, jnp.minimum((i + 1) * per, last_halo), 0, 0)),
            pl.BlockSpec(halo_shape, lambda b, i: (b, 0, jnp.where(i == nt - 1, 0, last_halo), 0, 0)),
            pl.BlockSpec((1, ts, e), lambda b, i: (b, i, 0)),
            h_spec,
            _resident((nch, CONF_KERNEL, STREAMS, LANES)),
            _resident((1, e)),
            _resident((1, e)),
            _resident((1, e)),
            _resident((e, d)),
        ],
        out_specs=h_spec,
        out_shape=jax.ShapeDtypeStruct((bsz, STREAMS, sl, d), F32),
        scratch_shapes=[pltpu.VMEM((ts, e), F32), pltpu.VMEM((ts, d), F32), slab],
        compiler_params=params,
        name="conformer_out",
    )(u, u, u, u, z, h4, w16, dw_b.reshape(1, e), ln_w.reshape(1, e), ln_b.reshape(1, e), w_out.astype(BF16))
    return out4.reshape(bsz, s, d)


def _halo_maps(ts, s):
    per = ts // HALO
    last = s // HALO - 1

    def prev_map(b, i):
        return (b, jnp.maximum(i * per - 1, 0), 0)

    def next_map(b, i):
        return (b, jnp.minimum((i + 1) * per, last), 0)

    return prev_map, next_map


def _ssd_in_kernel(hp_ref, hm_ref, hx_ref, nw_ref, wzx_ref, wdt_ref, cw_ref, cb_ref, dtb_ref,
                   z_ref, xbc_ref, dt_ref, u_scr, *, ts, lc):
    hn_all, valid = _normed_tile(hp_ref, hm_ref, hx_ref, nw_ref, ts)
    hn_m = hn_all[HALO:HALO + ts]
    n_chunks = SSD_CONV_DIM // lc

    def project(c, buf):
        off_w = pl.multiple_of(D_INNER + c * lc, lc)
        p = jnp.where(valid, _dot(hn_all, wzx_ref[:, pl.ds(off_w, lc)]), 0.0)
        for s in range(lc // LANES):
            u_scr[buf, s] = p[:, s * LANES:(s + 1) * LANES]

    def conv(c, buf):
        off = pl.multiple_of(c * lc, lc)
        w = cw_ref[:, pl.ds(off, lc)]
        b = cb_ref[:, pl.ds(off, lc)]

        def store(r0, acc):
            xbc_ref[0, r0:r0 + CONV_ROWS, pl.ds(off, lc)] = _silu(acc).astype(BF16)

        _dwconv_slabs(u_scr.at[buf], w, b, SSD_CONV, ts, store)

    project(0, 0)

    def pair(cp, carry):
        c0 = 2 * cp
        project(c0 + 1, 1)
        conv(c0, 0)
        project(jnp.minimum(c0 + 2, n_chunks - 1), 0)
        conv(c0 + 1, 1)
        return carry

    lax.fori_loop(0, n_chunks // 2, pair, 0)
    z_ref[0] = _dot(hn_m, wzx_ref[:, 0:D_INNER]).astype(BF16)
    dt_ref[0] = _softplus(_dot(hn_m, wdt_ref[...]) + dtb_ref[...])


def _ssd_in_layer(h, nw, w_zx, w_dt, conv_w, conv_b, dt_bias, *, ts=512, lc=512):
    bsz, s, d = h.shape
    e, cd = D_INNER, SSD_CONV_DIM
    assert s % ts == 0 and ts % CONV_ROWS == 0 and ts % HALO == 0 and (cd // lc) % 2 == 0
    prev_map, next_map = _halo_maps(ts, s)
    kern = functools.partial(_ssd_in_kernel, ts=ts, lc=lc)
    tile = lambda b, i: (b, i, 0)
    return pl.pallas_call(
        kern,
        grid=(bsz, s // ts),
        in_specs=[
            pl.BlockSpec((1, HALO, d), prev_map),
            pl.BlockSpec((1, ts, d), tile),
            pl.BlockSpec((1, HALO, d), next_map),
            _resident((1, d)),
            _resident((d, e + cd)),
            _resident((d, 2 * LANES)),
            _resident((SSD_CONV, cd)),
            _resident((1, cd)),
            _resident((1, 2 * LANES)),
        ],
        out_specs=[
            pl.BlockSpec((1, ts, e), tile),
            pl.BlockSpec((1, ts, cd), tile),
            pl.BlockSpec((1, ts, 2 * LANES), tile),
        ],
        out_shape=[
            jax.ShapeDtypeStruct((bsz, s, e), BF16),
            jax.ShapeDtypeStruct((bsz, s, cd), BF16),
            jax.ShapeDtypeStruct((bsz, s, 2 * LANES), F32),
        ],
        scratch_shapes=[pltpu.VMEM((2, lc // LANES, ts + 2 * HALO, LANES), F32)],
        compiler_params=pltpu.CompilerParams(
            dimension_semantics=("arbitrary", "arbitrary"), vmem_limit_bytes=VMEM_LIMIT),
        name="ssd_in_proj",
    )(h, h, h, nw.reshape(1, d), w_zx, w_dt, conv_w, conv_b.reshape(1, cd), dt_bias)


MASKED = -1e30


def _split2_cat(w):
    hi = w.astype(BF16)
    lo = (w - hi.astype(F32)).astype(BF16)
    return jnp.concatenate([hi, lo], axis=1)


def _scan_chunk(x, bm, cm, dt, a_neg, e2_ref, state_scr, y_scr, rev):
    L = CHUNK
    a = dt * a_neg
    r = lax.broadcasted_iota(jnp.int32, (L, L), 0)
    cdx = lax.broadcasted_iota(jnp.int32, (L, L), 1)
    tri = (cdx >= r) if rev else (cdx <= r)
    tri_b = jnp.where(tri, 1.0, 0.0).astype(BF16)
    a1, a2, a3 = _split3(a)
    acs = _dot(tri_b, a1) + _dot(tri_b, a2) + _dot(tri_b, a3)
    src_t = (acs - jnp.log(dt)).T
    last = acs[0:1, :] if rev else acs[L - 1:L, :]
    w3s = _split2_cat(jnp.exp(acs))
    w2s = _split2_cat(dt * jnp.exp(last - acs))
    lane = lax.broadcasted_iota(jnp.int32, (L, LANES), 1)
    lo_half = lane < HEADDIM

    for g in range(SSD_GROUPS):
        bg = bm[:, g * D_STATE:(g + 1) * D_STATE]
        cg = cm[:, g * D_STATE:(g + 1) * D_STATE]
        gsl = slice(g * GROUP_WIDTH, (g + 1) * GROUP_WIDTH)
        cb = lax.dot_general(cg, bg, (((1,), (1,)), ((), ())), preferred_element_type=F32)
        yds = []
        for qq in range(GROUP_WIDTH // LANES):
            q = g * (GROUP_WIDTH // LANES) + qq
            ms = []
            for hh in (2 * q, 2 * q + 1):
                d = acs[:, hh:hh + 1] - src_t[hh:hh + 1, :]
                ms.append((cb * jnp.exp(jnp.where(tri, d, MASKED))).astype(BF16))
            mcat = jnp.concatenate(ms, axis=1)
            xq = x[:, q * LANES:(q + 1) * LANES]
            zero = jnp.zeros_like(xq)
            xbd = jnp.concatenate([jnp.where(lo_half, xq, zero), jnp.where(lo_half, zero, xq)], axis=0)
            yds.append(_dot(mcat, xbd))
        w3e = _dot(w3s, e2_ref[:, gsl])
        w2e = _dot(w2s, e2_ref[:, gsl])
        st = state_scr[:, gsl]
        y_scr[:, gsl] = jnp.concatenate(yds, axis=1) + _dot(cg, st.astype(BF16)) * w3e
        xs = (x[:, gsl].astype(F32) * w2e).astype(BF16)
        bg_t = bg.astype(F32).T.astype(BF16)
        dec = w3e[0:1, :] if rev else w3e[L - 1:L, :]
        state_scr[:, gsl] = st * dec + _dot(bg_t, xs)


def _ssd_fwd_kernel(x_ref, b_ref, c_ref, dt_ref, alog_ref, e2_ref, y_ref, state_scr, y_scr, *, n_sub):
    @pl.when(pl.program_id(0) == 0)
    def _():
        state_scr[...] = jnp.zeros_like(state_scr)

    a_neg = -jnp.exp(alog_ref[...])

    def body(ci, carry):
        rows = pl.ds(pl.multiple_of(ci * CHUNK, CHUNK), CHUNK)
        for bi in range(x_ref.shape[0]):
            _scan_chunk(x_ref[bi, rows, :], b_ref[bi, rows, :], c_ref[bi, rows, :], dt_ref[bi, rows, :], a_neg,
                        e2_ref, state_scr.at[bi], y_scr.at[bi], False)
            y_ref[bi, rows, :] = y_scr[bi].astype(BF16)
        return carry

    lax.fori_loop(0, n_sub, body, 0)


def _ssd_bwd_out_kernel(x_ref, b_ref, c_ref, dt_ref, alog_ref, e2_ref, yf_ref, z_ref, h_ref, dexp_ref,
                        gnw_ref, wout_ref, fnw_ref, o_ref, state_scr, y_scr, g_scr, *, n_sub, final):
    @pl.when(pl.program_id(0) == 0)
    def _():
        state_scr[...] = jnp.zeros_like(state_scr)

    a_neg = -jnp.exp(alog_ref[...])
    bsz = x_ref.shape[0]
    ts = n_sub * CHUNK

    def body(k, carry):
        row0 = pl.multiple_of((n_sub - 1 - k) * CHUNK, CHUNK)
        rows = pl.ds(row0, CHUNK)
        for bi in range(bsz):
            x = x_ref[bi, rows, :]
            _scan_chunk(x, b_ref[bi, rows, :], c_ref[bi, rows, :], dt_ref[bi, rows, :], a_neg,
                        e2_ref, state_scr.at[bi], y_scr.at[bi], True)
            y = y_scr[bi] + yf_ref[bi, rows, :].astype(F32) + x.astype(F32) * dexp_ref[...]
            gated = y * _silu(z_ref[bi, rows, :].astype(F32))
            g_scr[pl.ds(bi * ts + row0, CHUNK), :] = _rms_bf16(gated, gnw_ref[...])
        return carry

    lax.fori_loop(0, n_sub, body, 0)
    res = _dot(g_scr[...], wout_ref[...])
    for bi in range(bsz):
        out = h_ref[bi] + res[bi * ts:(bi + 1) * ts]
        if final:
            ms = jnp.mean(out * out, axis=-1, keepdims=True)
            out = out * lax.rsqrt(ms + EPS) * fnw_ref[...]
        o_ref[bi] = out


def _ssd_scans(h, xbc, z, dt, alog_pad, e2, dexp, gnw, w_out, fnw, *, final, n_sub=2):
    bsz, s, d = h.shape
    e = D_INNER
    ts = n_sub * CHUNK
    assert s % ts == 0
    nblk = s // ts
    nb = D_INNER // (SSD_GROUPS * D_STATE)
    gw = SSD_GROUPS * D_STATE
    params = pltpu.CompilerParams(dimension_semantics=("arbitrary",), vmem_limit_bytes=VMEM_LIMIT)

    def resident(shape):
        nd = len(shape)
        return pl.BlockSpec(shape, lambda j: (0,) * nd, pipeline_mode=pl.Buffered(1))

    def scan_specs(blk, d_idx):
        return [
            pl.BlockSpec((bsz, ts, e), lambda j: (0, blk(j), 0)),
            pl.BlockSpec((bsz, ts, gw), lambda j: (0, blk(j), nb)),
            pl.BlockSpec((bsz, ts, gw), lambda j: (0, blk(j), nb + 1)),
            pl.BlockSpec((bsz, ts, LANES), lambda j: (0, blk(j), d_idx)),
            pl.BlockSpec((1, LANES), lambda j: (0, d_idx)),
            resident((2 * LANES, e)),
        ]

    scratch = [pltpu.VMEM((bsz, D_STATE, e), F32), pltpu.VMEM((bsz, CHUNK, e), F32)]
    fwd = lambda j: j
    yf = pl.pallas_call(
        functools.partial(_ssd_fwd_kernel, n_sub=n_sub),
        grid=(nblk,),
        in_specs=scan_specs(fwd, 0),
        out_specs=pl.BlockSpec((bsz, ts, e), lambda j: (0, j, 0)),
        out_shape=jax.ShapeDtypeStruct((bsz, s, e), BF16),
        scratch_shapes=scratch,
        compiler_params=params,
        name="ssd_scan_fwd",
    )(xbc, xbc, xbc, dt, alog_pad, e2)

    bwd = lambda j: nblk - 1 - j
    tile_b = lambda j: (0, bwd(j), 0)
    return pl.pallas_call(
        functools.partial(_ssd_bwd_out_kernel, n_sub=n_sub, final=final),
        grid=(nblk,),
        in_specs=scan_specs(bwd, 1) + [
            pl.BlockSpec((bsz, ts, e), tile_b),
            pl.BlockSpec((bsz, ts, e), tile_b),
            pl.BlockSpec((bsz, ts, d), tile_b),
            resident((1, e)),
            resident((1, e)),
            resident((e, d)),
            resident((1, d)),
        ],
        out_specs=pl.BlockSpec((bsz, ts, d), tile_b),
        out_shape=jax.ShapeDtypeStruct((bsz, s, d), F32),
        scratch_shapes=scratch + [pltpu.VMEM((bsz * ts, e), BF16)],
        compiler_params=params,
        name="ssd_scan_bwd_out",
    )(xbc, xbc, xbc, dt, alog_pad, e2, yf, z, h, dexp, gnw.reshape(1, e), w_out.astype(BF16), fnw.reshape(1, d))


def _pad_lanes(v):
    return jnp.pad(v, ((0, 0), (0, LANES - SSD_HEADS))).reshape(1, 2 * LANES)


def _expand_matrix():
    r = jnp.arange(2 * LANES)[:, None] % LANES
    c = jnp.arange(D_INNER)[None, :] // HEADDIM
    return (r == c).astype(BF16)


def _ssd_layer(h, nw, w_in, conv_w, conv_b, dt_bias, a_log, d_skip, gnw, w_out, fnw, *, final):
    d = h.shape[-1]
    e, cd = D_INNER, SSD_CONV_DIM
    w_zx = w_in[:, :e + cd].astype(BF16)
    w_dt = jnp.pad(w_in[:, e + cd:].reshape(d, 2, SSD_HEADS),
                   ((0, 0), (0, 0), (0, LANES - SSD_HEADS))).reshape(d, 2 * LANES).astype(BF16)
    z, xbc, dt = _ssd_in_layer(h, nw, w_zx, w_dt, conv_w, conv_b, _pad_lanes(dt_bias))
    dexp = jnp.repeat(d_skip, HEADDIM).reshape(1, e)
    return _ssd_scans(h, xbc, z, dt, _pad_lanes(a_log), _expand_matrix(), dexp, gnw, w_out, fnw, final=final)


def kernel(x, norm_w, final_norm_w, cm_w_in, cm_dw_w, cm_dw_b, cm_ln_w, cm_ln_b, cm_w_out, ssd_w_in,
           ssd_conv_w, ssd_conv_b, ssd_dt_bias, ssd_A_log, ssd_D, ssd_norm_w, ssd_w_out):
    depth = norm_w.shape[0]
    assert depth % 2 == 0
    h = x
    for i in range(depth):
        j = i // 2
        if i % 2 == 0:
            h = _conformer_layer(h, norm_w[i], cm_w_in[j], cm_dw_w[j], cm_dw_b[j], cm_ln_w[j],
                                 cm_ln_b[j], cm_w_out[j])
        else:
            h = _ssd_layer(h, norm_w[i], ssd_w_in[j], ssd_conv_w[j], ssd_conv_b[j], ssd_dt_bias[j],
                           ssd_A_log[j], ssd_D[j], ssd_norm_w[j], ssd_w_out[j], final_norm_w,
                           final=(i == depth - 1))
    return h
```

```python
import functools

import jax
import jax.numpy as jnp
from jax import lax
from jax.experimental import pallas as pl
from jax.experimental.pallas import tpu as pltpu

F32 = jnp.float32
BF16 = jnp.bfloat16

EPS = 1e-5
D_MODEL = 1024
D_INNER = 2048
CONF_KERNEL = 31
HEADDIM = 64
SSD_HEADS = 32
SSD_GROUPS = 4
D_STATE = 128
SSD_CONV = 5
CHUNK = 128
SSD_CONV_DIM = D_INNER + 2 * SSD_GROUPS * D_STATE
GROUP_WIDTH = D_INNER // SSD_GROUPS
LANES = 128
HALO = 16
CONV_ROWS = 32
VMEM_LIMIT = 56 * 1024 * 1024


def _sigmoid(x):
    return 1.0 / (1.0 + jnp.exp(-x))


def _silu(x):
    return x * _sigmoid(x)


def _softplus(x):
    return jnp.maximum(x, 0.0) + jnp.log1p(jnp.exp(-jnp.abs(x)))


def _rms_bf16(x, w):
    ms = jnp.mean(x * x, axis=-1, keepdims=True)
    return (x * lax.rsqrt(ms + EPS) * w).astype(BF16)


def _dot(a, b):
    return jnp.dot(a, b, preferred_element_type=F32)


def _split3(a):
    a1 = a.astype(BF16)
    r1 = a - a1.astype(F32)
    a2 = r1.astype(BF16)
    a3 = (r1 - a2.astype(F32)).astype(BF16)
    return a1, a2, a3


def _normed_tile(hp_ref, hm_ref, hx_ref, nw_ref, ts):
    i = pl.program_id(1)
    nt = pl.num_programs(1)
    hall = jnp.concatenate([hp_ref[0], hm_ref[0], hx_ref[0]], axis=0)
    hn_all = _rms_bf16(hall, nw_ref[...])
    rows = lax.broadcasted_iota(jnp.int32, (ts + 2 * HALO, 1), 0)
    lo = jnp.where(i == 0, HALO, 0)
    hi = jnp.where(i == nt - 1, HALO + ts, ts + 2 * HALO)
    valid = (rows >= lo) & (rows < hi)
    return hn_all, valid


def _dwconv_slabs(u_ref, w, b, width, ts, store):
    pad = (width - 1) // 2
    for rb in range(ts // CONV_ROWS):
        r0 = rb * CONV_ROWS
        accs = []
        for s in range(u_ref.shape[0]):
            lanes = slice(s * LANES, (s + 1) * LANES)
            acc = jnp.broadcast_to(b[:, lanes], (CONV_ROWS, LANES))
            for k in range(width):
                acc = acc + w[k:k + 1, lanes] * u_ref[s, pl.ds(r0 + HALO - pad + k, CONV_ROWS), :]
            accs.append(acc)
        store(r0, jnp.concatenate(accs, axis=1))


STREAMS = 16
PITCH_PAD = 8
SLABS = D_MODEL // LANES
CONF_PAD = (CONF_KERNEL - 1) // 2


def _conf_in_kernel(h_ref, nw_ref, win_ref, u_ref, z_ref, slab_scr, hnp_scr, *, p_rows, lc):
    pitch = p_rows + PITCH_PAD
    nw = nw_ref[...]
    for j in range(STREAMS):
        hj = h_ref[0, j]
        hn = hj * lax.rsqrt(jnp.mean(hj * hj, axis=-1, keepdims=True) + EPS) * nw
        for s in range(SLABS):
            slab_scr[s, j * pitch:j * pitch + p_rows, :] = hn[:, s * LANES:(s + 1) * LANES]

    def permute_in(r, carry):
        row0 = pl.multiple_of(r * STREAMS, STREAMS)
        for s in range(SLABS):
            piece = slab_scr[s, pl.ds(r, STREAMS, stride=pitch), :]
            hnp_scr[pl.ds(row0, STREAMS), s * LANES:(s + 1) * LANES] = piece.astype(BF16)
        return carry

    lax.fori_loop(0, p_rows, permute_in, 0)

    def chunk(c, carry):
        off_v = pl.multiple_of(c * lc, lc)
        off_g = pl.multiple_of(D_INNER + c * lc, lc)
        off_z = pl.multiple_of(2 * D_INNER + c * lc, lc)
        lhs = hnp_scr[...]
        u = _dot(lhs, win_ref[:, pl.ds(off_v, lc)]) * _sigmoid(_dot(lhs, win_ref[:, pl.ds(off_g, lc)]))
        for q in range(lc // LANES):
            u_ref[0, c * (lc // LANES) + q] = u[:, q * LANES:(q + 1) * LANES].reshape(
                p_rows, STREAMS, LANES).astype(BF16)
        z_ref[0, :, pl.ds(off_v, lc)] = _dot(lhs, win_ref[:, pl.ds(off_z, lc)]).astype(BF16)
        return carry

    lax.fori_loop(0, D_INNER // lc, chunk, 0)


def _conf_out_kernel(up_ref, um_ref, ux_ref, wrap_ref, z_ref, h_ref, w16_ref, dwb_ref, lnw_ref, lnb_ref,
                     wout_ref, o_ref, c_scr, res_scr, slab_scr, *, p_rows):
    lc = LANES
    pitch = p_rows + PITCH_PAD
    i = pl.program_id(1)
    nt = pl.num_programs(1)

    def u_row(c, idx):
        if idx < 0:
            return up_ref[0, c, HALO + idx]
        if idx >= p_rows:
            return ux_ref[0, c, idx - p_rows]
        return um_ref[0, c, idx]

    def chunk(c, carry):
        off = pl.multiple_of(c * lc, lc)
        lanes = pl.ds(off, lc)
        b = dwb_ref[:, lanes]
        for r in range(p_rows):
            acc = jnp.zeros((STREAMS, lc), F32)
            for k in range(CONF_KERNEL):
                acc = acc + u_row(c, r + k - CONF_PAD).astype(F32) * w16_ref[c, k].astype(F32)
            c_scr[r * STREAMS:(r + 1) * STREAMS, lanes] = acc + b

        def fix_edge(rows, true_row):
            for r in rows:
                acc = jnp.broadcast_to(b, (STREAMS, lc))
                for k in range(CONF_KERNEL):
                    acc = acc + true_row(r + k - CONF_PAD) * w16_ref[c, k].astype(F32)
                c_scr[r * STREAMS:(r + 1) * STREAMS, lanes] = acc

        zero_row = jnp.zeros((1, lc), F32)

        @pl.when(i == 0)
        def _():
            def true_row(idx):
                if idx >= 0:
                    return um_ref[0, c, idx].astype(F32)
                prev_stream = wrap_ref[0, c, HALO + idx].astype(F32)
                return jnp.concatenate([zero_row, prev_stream[0:STREAMS - 1]], axis=0)
            fix_edge(range(CONF_PAD), true_row)

        @pl.when(i == nt - 1)
        def _():
            def true_row(idx):
                if idx < p_rows:
                    return um_ref[0, c, idx].astype(F32)
                next_stream = wrap_ref[0, c, idx - p_rows].astype(F32)
                return jnp.concatenate([next_stream[1:STREAMS], zero_row], axis=0)
            fix_edge(range(p_rows - CONF_PAD, p_rows), true_row)

        return carry

    lax.fori_loop(0, D_INNER // lc, chunk, 0)

    cv = c_scr[...]
    mu = jnp.mean(cv, axis=-1, keepdims=True)
    xc = cv - mu
    var = jnp.mean(xc * xc, axis=-1, keepdims=True)
    y = xc * lax.rsqrt(var + EPS) * lnw_ref[...] + lnb_ref[...]
    gate = (_silu(y) * _silu(z_ref[0].astype(F32))).astype(BF16)
    res_scr[...] = _dot(gate, wout_ref[...])

    def permute_out(r, carry):
        row0 = pl.multiple_of(r * STREAMS, STREAMS)
        for s in range(SLABS):
            slab_scr[s, pl.ds(r, STREAMS, stride=pitch), :] = res_scr[pl.ds(row0, STREAMS),
                                                                     s * LANES:(s + 1) * LANES]
        return carry

    lax.fori_loop(0, p_rows, permute_out, 0)
    for j in range(STREAMS):
        for s in range(SLABS):
            lanes = slice(s * LANES, (s + 1) * LANES)
            o_ref[0, j, :, lanes] = h_ref[0, j, :, lanes] + slab_scr[s, j * pitch:j * pitch + p_rows, :]


def _resident(shape):
    nd = len(shape)
    return pl.BlockSpec(shape, lambda b, i: (0,) * nd, pipeline_mode=pl.Buffered(1))


def _conformer_layer(h, nw, w_in, dw_w, dw_b, ln_w, ln_b, w_out, *, p_rows=32, lc_in=512):
    bsz, s, d = h.shape
    e = D_INNER
    sl = s // STREAMS
    assert s % STREAMS == 0 and sl % p_rows == 0 and p_rows % HALO == 0 and p_rows >= CONF_PAD
    nt = sl // p_rows
    ts = p_rows * STREAMS
    nch = e // LANES
    h4 = h.reshape(bsz, STREAMS, sl, d)
    params = pltpu.CompilerParams(dimension_semantics=("arbitrary", "arbitrary"), vmem_limit_bytes=VMEM_LIMIT)
    slab = pltpu.VMEM((SLABS, STREAMS * (p_rows + PITCH_PAD), LANES), F32)
    h_spec = pl.BlockSpec((1, STREAMS, p_rows, d), lambda b, i: (b, 0, i, 0))

    u, z = pl.pallas_call(
        functools.partial(_conf_in_kernel, p_rows=p_rows, lc=lc_in),
        grid=(bsz, nt),
        in_specs=[h_spec, _resident((1, d)), _resident((d, 3 * e))],
        out_specs=[pl.BlockSpec((1, nch, p_rows, STREAMS, LANES), lambda b, i: (b, 0, i, 0, 0)),
                   pl.BlockSpec((1, ts, e), lambda b, i: (b, i, 0))],
        out_shape=[jax.ShapeDtypeStruct((bsz, nch, sl, STREAMS, LANES), BF16),
                   jax.ShapeDtypeStruct((bsz, sl * STREAMS, e), BF16)],
        scratch_shapes=[slab, pltpu.VMEM((ts, d), BF16)],
        compiler_params=params,
        name="conformer_in",
    )(h4, nw.reshape(1, d), w_in.astype(BF16))

    per = p_rows // HALO
    last_halo = sl // HALO - 1
    w16 = jnp.broadcast_to(dw_w.astype(BF16).reshape(CONF_KERNEL, 1, nch, LANES),
                           (CONF_KERNEL, STREAMS, nch, LANES)).transpose(2, 0, 1, 3)
    halo_shape = (1, nch, HALO, STREAMS, LANES)
    out4 = pl.pallas_call(
        functools.partial(_conf_out_kernel, p_rows=p_rows),
        grid=(bsz, nt),
        in_specs=[
            pl.BlockSpec(halo_shape, lambda b, i: (b, 0, jnp.maximum(i * per - 1, 0), 0, 0)),
            pl.BlockSpec((1, nch, p_rows, STREAMS, LANES), lambda b, i: (b, 0, i, 0, 0)),
            pl.BlockSpec(halo_shape, lambda b, i: (b, 0, jnp.minimum((i + 1) * per, last_halo), 0, 0)),
            pl.BlockSpec(halo_shape, lambda b, i: (b, 0, jnp.where(i == nt - 1, 0, last_halo), 0, 0)),
            pl.BlockSpec((1, ts, e), lambda b, i: (b, i, 0)),
            h_spec,
            _resident((nch, CONF_KERNEL, STREAMS, LANES)),
            _resident((1, e)),
            _resident((1, e)),
            _resident((1, e)),
            _resident((e, d)),
        ],
        out_specs=h_spec,
        out_shape=jax.ShapeDtypeStruct((bsz, STREAMS, sl, d), F32),
        scratch_shapes=[pltpu.VMEM((ts, e), F32), pltpu.VMEM((ts, d), F32), slab],
        compiler_params=params,
        name="conformer_out",
    )(u, u, u, u, z, h4, w16, dw_b.reshape(1, e), ln_w.reshape(1, e), ln_b.reshape(1, e), w_out.astype(BF16))
    return out4.reshape(bsz, s, d)


def _halo_maps(ts, s):
    per = ts // HALO
    last = s // HALO - 1

    def prev_map(b, i):
        return (b, jnp.maximum(i * per - 1, 0), 0)

    def next_map(b, i):
        return (b, jnp.minimum((i + 1) * per, last), 0)

    return prev_map, next_map


def _ssd_in_kernel(hp_ref, hm_ref, hx_ref, nw_ref, wzx_ref, wdt_ref, cw_ref, cb_ref, dtb_ref,
                   z_ref, xbc_ref, dt_ref, u_scr, *, ts, lc):
    hn_all, valid = _normed_tile(hp_ref, hm_ref, hx_ref, nw_ref, ts)
    hn_m = hn_all[HALO:HALO + ts]
    n_chunks = SSD_CONV_DIM // lc

    def project(c, buf):
        off_w = pl.multiple_of(D_INNER + c * lc, lc)
        p = jnp.where(valid, _dot(hn_all, wzx_ref[:, pl.ds(off_w, lc)]), 0.0)
        for s in range(lc // LANES):
            u_scr[buf, s] = p[:, s * LANES:(s + 1) * LANES]

    def conv(c, buf):
        off = pl.multiple_of(c * lc, lc)
        w = cw_ref[:, pl.ds(off, lc)]
        b = cb_ref[:, pl.ds(off, lc)]

        def store(r0, acc):
            xbc_ref[0, r0:r0 + CONV_ROWS, pl.ds(off, lc)] = _silu(acc).astype(BF16)

        _dwconv_slabs(u_scr.at[buf], w, b, SSD_CONV, ts, store)

    project(0, 0)

    def pair(cp, carry):
        c0 = 2 * cp
        project(c0 + 1, 1)
        conv(c0, 0)
        project(jnp.minimum(c0 + 2, n_chunks - 1), 0)
        conv(c0 + 1, 1)
        return carry

    lax.fori_loop(0, n_chunks // 2, pair, 0)
    z_ref[0] = _dot(hn_m, wzx_ref[:, 0:D_INNER]).astype(BF16)
    dt_ref[0] = _softplus(_dot(hn_m, wdt_ref[...]) + dtb_ref[...])


def _ssd_in_layer(h, nw, w_zx, w_dt, conv_w, conv_b, dt_bias, *, ts=512, lc=512):
    bsz, s, d = h.shape
    e, cd = D_INNER, SSD_CONV_DIM
    assert s % ts == 0 and ts % CONV_ROWS == 0 and ts % HALO == 0 and (cd // lc) % 2 == 0
    prev_map, next_map = _halo_maps(ts, s)
    kern = functools.partial(_ssd_in_kernel, ts=ts, lc=lc)
    tile = lambda b, i: (b, i, 0)
    return pl.pallas_call(
        kern,
        grid=(bsz, s // ts),
        in_specs=[
            pl.BlockSpec((1, HALO, d), prev_map),
            pl.BlockSpec((1, ts, d), tile),
            pl.BlockSpec((1, HALO, d), next_map),
            _resident((1, d)),
            _resident((d, e + cd)),
            _resident((d, 2 * LANES)),
            _resident((SSD_CONV, cd)),
            _resident((1, cd)),
            _resident((1, 2 * LANES)),
        ],
        out_specs=[
            pl.BlockSpec((1, ts, e), tile),
            pl.BlockSpec((1, ts, cd), tile),
            pl.BlockSpec((1, ts, 2 * LANES), tile),
        ],
        out_shape=[
            jax.ShapeDtypeStruct((bsz, s, e), BF16),
            jax.ShapeDtypeStruct((bsz, s, cd), BF16),
            jax.ShapeDtypeStruct((bsz, s, 2 * LANES), F32),
        ],
        scratch_shapes=[pltpu.VMEM((2, lc // LANES, ts + 2 * HALO, LANES), F32)],
        compiler_params=pltpu.CompilerParams(
            dimension_semantics=("arbitrary", "arbitrary"), vmem_limit_bytes=VMEM_LIMIT),
        name="ssd_in_proj",
    )(h, h, h, nw.reshape(1, d), w_zx, w_dt, conv_w, conv_b.reshape(1, cd), dt_bias)


MASKED = -1e30


def _split2_cat(w):
    hi = w.astype(BF16)
    lo = (w - hi.astype(F32)).astype(BF16)
    return jnp.concatenate([hi, lo], axis=1)


def _scan_prepare(dt, a_neg, rev):
    L = CHUNK
    a = dt * a_neg
    r = lax.broadcasted_iota(jnp.int32, (L, L), 0)
    cdx = lax.broadcasted_iota(jnp.int32, (L, L), 1)
    tri = (cdx >= r) if rev else (cdx <= r)
    tri_b = jnp.where(tri, 1.0, 0.0).astype(BF16)
    a1, a2, a3 = _split3(a)
    acs = _dot(tri_b, a1) + _dot(tri_b, a2) + _dot(tri_b, a3)
    src_t = (acs - jnp.log(dt)).T
    last = acs[0:1, :] if rev else acs[L - 1:L, :]
    w3s = _split2_cat(jnp.exp(acs))
    w2s = _split2_cat(dt * jnp.exp(last - acs))
    return tri, acs, src_t, w3s, w2s


def _scan_chunks(xs, bms, cms, dts, a_neg, e2_ref, state_scr, emit, rev):
    L = CHUNK
    nb = len(xs)
    prep = [_scan_prepare(dt, a_neg, rev) for dt in dts]
    lane = lax.broadcasted_iota(jnp.int32, (L, LANES), 1)
    lo_half = lane < HEADDIM
    wcat = jnp.concatenate([w for p in prep for w in (p[3], p[4])], axis=0)

    for g in range(SSD_GROUPS):
        gsl = slice(g * GROUP_WIDTH, (g + 1) * GROUP_WIDTH)
        wexp = _dot(wcat, e2_ref[:, gsl])
        for bi in range(nb):
            tri, acs, src_t = prep[bi][0], prep[bi][1], prep[bi][2]
            x = xs[bi]
            bg = bms[bi][:, g * D_STATE:(g + 1) * D_STATE]
            cg = cms[bi][:, g * D_STATE:(g + 1) * D_STATE]
            cb = lax.dot_general(cg, bg, (((1,), (1,)), ((), ())), preferred_element_type=F32)
            yds = []
            for qq in range(GROUP_WIDTH // LANES):
                q = g * (GROUP_WIDTH // LANES) + qq
                ms = []
                for hh in (2 * q, 2 * q + 1):
                    d = acs[:, hh:hh + 1] - src_t[hh:hh + 1, :]
                    ms.append((cb * jnp.exp(jnp.where(tri, d, MASKED))).astype(BF16))
                mcat = jnp.concatenate(ms, axis=1)
                xq = x[:, q * LANES:(q + 1) * LANES]
                zero = jnp.zeros_like(xq)
                xbd = jnp.concatenate([jnp.where(lo_half, xq, zero), jnp.where(lo_half, zero, xq)], axis=0)
                yds.append(_dot(mcat, xbd))
            w3e = wexp[bi * 2 * L:bi * 2 * L + L]
            w2e = wexp[bi * 2 * L + L:(bi + 1) * 2 * L]
            st = state_scr[bi, :, gsl]
            emit(bi, gsl, jnp.concatenate(yds, axis=1) + _dot(cg, st.astype(BF16)) * w3e)
            xsc = (x[:, gsl].astype(F32) * w2e).astype(BF16)
            bg_t = bg.astype(F32).T.astype(BF16)
            dec = w3e[0:1, :] if rev else w3e[L - 1:L, :]
            state_scr[bi, :, gsl] = st * dec + _dot(bg_t, xsc)


def _ssd_fwd_kernel(x_ref, b_ref, c_ref, dt_ref, alog_ref, e2_ref, y_ref, state_scr, *, n_sub):
    @pl.when(pl.program_id(0) == 0)
    def _():
        state_scr[...] = jnp.zeros_like(state_scr)

    a_neg = -jnp.exp(alog_ref[...])

    def body(ci, carry):
        rows = pl.ds(pl.multiple_of(ci * CHUNK, CHUNK), CHUNK)
        nb = x_ref.shape[0]

        def emit(bi, lanes, y):
            y_ref[bi, rows, lanes] = y.astype(BF16)

        _scan_chunks([x_ref[bi, rows, :] for bi in range(nb)], [b_ref[bi, rows, :] for bi in range(nb)],
                     [c_ref[bi, rows, :] for bi in range(nb)], [dt_ref[bi, rows, :] for bi in range(nb)],
                     a_neg, e2_ref, state_scr, emit, False)
        return carry

    lax.fori_loop(0, n_sub, body, 0)


def _ssd_bwd_out_kernel(x_ref, b_ref, c_ref, dt_ref, alog_ref, e2_ref, yf_ref, z_ref, h_ref, dexp_ref,
                        gnw_ref, wout_ref, fnw_ref, o_ref, state_scr, y_scr, *, n_sub, final):
    @pl.when(pl.program_id(0) == 0)
    def _():
        state_scr[...] = jnp.zeros_like(state_scr)

    a_neg = -jnp.exp(alog_ref[...])
    bsz = x_ref.shape[0]
    ts = n_sub * CHUNK

    def body(k, carry):
        row0 = pl.multiple_of((n_sub - 1 - k) * CHUNK, CHUNK)
        rows = pl.ds(row0, CHUNK)
        xs = [x_ref[bi, rows, :] for bi in range(bsz)]
        sumsq = [jnp.zeros((CHUNK, 1), F32) for _ in range(bsz)]

        def emit(bi, lanes, y_bwd):
            y = y_bwd + yf_ref[bi, rows, lanes].astype(F32) + xs[bi][:, lanes].astype(F32) * dexp_ref[:, lanes]
            gated = y * _silu(z_ref[bi, rows, lanes].astype(F32))
            y_scr[bi, :, lanes] = gated
            sumsq[bi] = sumsq[bi] + jnp.sum(gated * gated, axis=-1, keepdims=True)

        _scan_chunks(xs, [b_ref[bi, rows, :] for bi in range(bsz)], [c_ref[bi, rows, :] for bi in range(bsz)],
                     [dt_ref[bi, rows, :] for bi in range(bsz)], a_neg, e2_ref, state_scr, emit, True)
        gs = []
        for bi in range(bsz):
            inv = lax.rsqrt(sumsq[bi] * (1.0 / D_INNER) + EPS)
            gs.append((y_scr[bi] * inv * gnw_ref[...]).astype(BF16))
        res = _dot(jnp.concatenate(gs, axis=0), wout_ref[...])
        for bi in range(bsz):
            out = h_ref[bi, rows, :] + res[bi * CHUNK:(bi + 1) * CHUNK]
            if final:
                ms = jnp.mean(out * out, axis=-1, keepdims=True)
                out = out * lax.rsqrt(ms + EPS) * fnw_ref[...]
            o_ref[bi, rows, :] = out
        return carry

    lax.fori_loop(0, n_sub, body, 0)


def _ssd_scans(h, xbc, z, dt, alog_pad, e2, dexp, gnw, w_out, fnw, *, final, n_sub=2):
    bsz, s, d = h.shape
    e = D_INNER
    ts = n_sub * CHUNK
    assert s % ts == 0
    nblk = s // ts
    nb = D_INNER // (SSD_GROUPS * D_STATE)
    gw = SSD_GROUPS * D_STATE
    params = pltpu.CompilerParams(dimension_semantics=("arbitrary",), vmem_limit_bytes=VMEM_LIMIT)

    def resident(shape):
        nd = len(shape)
        return pl.BlockSpec(shape, lambda j: (0,) * nd, pipeline_mode=pl.Buffered(1))

    def scan_specs(blk, d_idx):
        return [
            pl.BlockSpec((bsz, ts, e), lambda j: (0, blk(j), 0)),
            pl.BlockSpec((bsz, ts, gw), lambda j: (0, blk(j), nb)),
            pl.BlockSpec((bsz, ts, gw), lambda j: (0, blk(j), nb + 1)),
            pl.BlockSpec((bsz, ts, LANES), lambda j: (0, blk(j), d_idx)),
            pl.BlockSpec((1, LANES), lambda j: (0, d_idx)),
            resident((2 * LANES, e)),
        ]

    state = pltpu.VMEM((bsz, D_STATE, e), F32)
    fwd = lambda j: j
    yf = pl.pallas_call(
        functools.partial(_ssd_fwd_kernel, n_sub=n_sub),
        grid=(nblk,),
        in_specs=scan_specs(fwd, 0),
        out_specs=pl.BlockSpec((bsz, ts, e), lambda j: (0, j, 0)),
        out_shape=jax.ShapeDtypeStruct((bsz, s, e), BF16),
        scratch_shapes=[state],
        compiler_params=params,
        name="ssd_scan_fwd",
    )(xbc, xbc, xbc, dt, alog_pad, e2)

    bwd = lambda j: nblk - 1 - j
    tile_b = lambda j: (0, bwd(j), 0)
    return pl.pallas_call(
        functools.partial(_ssd_bwd_out_kernel, n_sub=n_sub, final=final),
        grid=(nblk,),
        in_specs=scan_specs(bwd, 1) + [
            pl.BlockSpec((bsz, ts, e), tile_b),
            pl.BlockSpec((bsz, ts, e), tile_b),
            pl.BlockSpec((bsz, ts, d), tile_b),
            resident((1, e)),
            resident((1, e)),
            resident((e, d)),
            resident((1, d)),
        ],
        out_specs=pl.BlockSpec((bsz, ts, d), tile_b),
        out_shape=jax.ShapeDtypeStruct((bsz, s, d), F32),
        scratch_shapes=[state, pltpu.VMEM((bsz, CHUNK, e), F32)],
        compiler_params=params,
        name="ssd_scan_bwd_out",
    )(xbc, xbc, xbc, dt, alog_pad, e2, yf, z, h, dexp, gnw.reshape(1, e), w_out.astype(BF16), fnw.reshape(1, d))


def _pad_lanes(v):
    return jnp.pad(v, ((0, 0), (0, LANES - SSD_HEADS))).reshape(1, 2 * LANES)


def _expand_matrix():
    r = jnp.arange(2 * LANES)[:, None] % LANES
    c = jnp.arange(D_INNER)[None, :] // HEADDIM
    return (r == c).astype(BF16)


def _ssd_layer(h, nw, w_in, conv_w, conv_b, dt_bias, a_log, d_skip, gnw, w_out, fnw, *, final):
    d = h.shape[-1]
    e, cd = D_INNER, SSD_CONV_DIM
    w_zx = w_in[:, :e + cd].astype(BF16)
    w_dt = jnp.pad(w_in[:, e + cd:].reshape(d, 2, SSD_HEADS),
                   ((0, 0), (0, 0), (0, LANES - SSD_HEADS))).reshape(d, 2 * LANES).astype(BF16)
    z, xbc, dt = _ssd_in_layer(h, nw, w_zx, w_dt, conv_w, conv_b, _pad_lanes(dt_bias))
    dexp = jnp.repeat(d_skip, HEADDIM).reshape(1, e)
    return _ssd_scans(h, xbc, z, dt, _pad_lanes(a_log), _expand_matrix(), dexp, gnw, w_out, fnw, final=final)


def kernel(x, norm_w, final_norm_w, cm_w_in, cm_dw_w, cm_dw_b, cm_ln_w, cm_ln_b, cm_w_out, ssd_w_in,
           ssd_conv_w, ssd_conv_b, ssd_dt_bias, ssd_A_log, ssd_D, ssd_norm_w, ssd_w_out):
    depth = norm_w.shape[0]
    assert depth % 2 == 0
    h = x
    for i in range(depth):
        j = i // 2
        if i % 2 == 0:
            h = _conformer_layer(h, norm_w[i], cm_w_in[j], cm_dw_w[j], cm_dw_b[j], cm_ln_w[j],
                                 cm_ln_b[j], cm_w_out[j])
        else:
            h = _ssd_layer(h, norm_w[i], ssd_w_in[j], ssd_conv_w[j], ssd_conv_b[j], ssd_dt_bias[j],
                           ssd_A_log[j], ssd_D[j], ssd_norm_w[j], ssd_w_out[j], final_norm_w,
                           final=(i == depth - 1))
    return h
```

```python
import functools

import jax
import jax.numpy as jnp
from jax import lax
from jax.experimental import pallas as pl
from jax.experimental.pallas import tpu as pltpu

F32 = jnp.float32
BF16 = jnp.bfloat16

EPS = 1e-5
D_MODEL = 1024
D_INNER = 2048
CONF_KERNEL = 31
HEADDIM = 64
SSD_HEADS = 32
SSD_GROUPS = 4
D_STATE = 128
SSD_CONV = 5
CHUNK = 128
SSD_CONV_DIM = D_INNER + 2 * SSD_GROUPS * D_STATE
GROUP_WIDTH = D_INNER // SSD_GROUPS
LANES = 128
HALO = 16
CONV_ROWS = 32
VMEM_LIMIT = 56 * 1024 * 1024


def _sigmoid(x):
    return 1.0 / (1.0 + jnp.exp(-x))


def _silu(x):
    return x * _sigmoid(x)


def _softplus(x):
    return jnp.maximum(x, 0.0) + jnp.log1p(jnp.exp(-jnp.abs(x)))


def _rms_bf16(x, w):
    ms = jnp.mean(x * x, axis=-1, keepdims=True)
    return (x * lax.rsqrt(ms + EPS) * w).astype(BF16)


def _dot(a, b):
    return jnp.dot(a, b, preferred_element_type=F32)


def _split3(a):
    a1 = a.astype(BF16)
    r1 = a - a1.astype(F32)
    a2 = r1.astype(BF16)
    a3 = (r1 - a2.astype(F32)).astype(BF16)
    return a1, a2, a3


def _normed_tile(hp_ref, hm_ref, hx_ref, nw_ref, ts):
    i = pl.program_id(1)
    nt = pl.num_programs(1)
    hall = jnp.concatenate([hp_ref[0], hm_ref[0], hx_ref[0]], axis=0)
    hn_all = _rms_bf16(hall, nw_ref[...])
    rows = lax.broadcasted_iota(jnp.int32, (ts + 2 * HALO, 1), 0)
    lo = jnp.where(i == 0, HALO, 0)
    hi = jnp.where(i == nt - 1, HALO + ts, ts + 2 * HALO)
    valid = (rows >= lo) & (rows < hi)
    return hn_all, valid


def _dwconv_slabs(u_ref, w, b, width, ts, store):
    pad = (width - 1) // 2
    for rb in range(ts // CONV_ROWS):
        r0 = rb * CONV_ROWS
        accs = []
        for s in range(u_ref.shape[0]):
            lanes = slice(s * LANES, (s + 1) * LANES)
            acc = jnp.broadcast_to(b[:, lanes], (CONV_ROWS, LANES))
            for k in range(width):
                acc = acc + w[k:k + 1, lanes] * u_ref[s, pl.ds(r0 + HALO - pad + k, CONV_ROWS), :]
            accs.append(acc)
        store(r0, jnp.concatenate(accs, axis=1))


STREAMS = 16
PITCH_PAD = 8
SLABS = D_MODEL // LANES
CONF_PAD = (CONF_KERNEL - 1) // 2


def _conf_in_kernel(h_ref, nw_ref, win_ref, u_ref, z_ref, slab_scr, hnp_scr, *, p_rows, lc):
    pitch = p_rows + PITCH_PAD
    nw = nw_ref[...]
    for j in range(STREAMS):
        hj = h_ref[0, j]
        hn = hj * lax.rsqrt(jnp.mean(hj * hj, axis=-1, keepdims=True) + EPS) * nw
        for s in range(SLABS):
            slab_scr[s, j * pitch:j * pitch + p_rows, :] = hn[:, s * LANES:(s + 1) * LANES]

    def permute_in(r, carry):
        row0 = pl.multiple_of(r * STREAMS, STREAMS)
        for s in range(SLABS):
            piece = slab_scr[s, pl.ds(r, STREAMS, stride=pitch), :]
            hnp_scr[pl.ds(row0, STREAMS), s * LANES:(s + 1) * LANES] = piece.astype(BF16)
        return carry

    lax.fori_loop(0, p_rows, permute_in, 0)

    def chunk(c, carry):
        off_v = pl.multiple_of(c * lc, lc)
        off_g = pl.multiple_of(D_INNER + c * lc, lc)
        off_z = pl.multiple_of(2 * D_INNER + c * lc, lc)
        lhs = hnp_scr[...]
        u = _dot(lhs, win_ref[:, pl.ds(off_v, lc)]) * _sigmoid(_dot(lhs, win_ref[:, pl.ds(off_g, lc)]))
        for q in range(lc // LANES):
            u_ref[0, c * (lc // LANES) + q] = u[:, q * LANES:(q + 1) * LANES].reshape(
                p_rows, STREAMS, LANES).astype(BF16)
        z_ref[0, :, pl.ds(off_v, lc)] = _dot(lhs, win_ref[:, pl.ds(off_z, lc)]).astype(BF16)
        return carry

    lax.fori_loop(0, D_INNER // lc, chunk, 0)


def _conf_out_kernel(up_ref, um_ref, ux_ref, wrap_ref, z_ref, h_ref, w16_ref, dwb_ref, lnw_ref, lnb_ref,
                     wout_ref, o_ref, c_scr, res_scr, slab_scr, *, p_rows):
    lc = LANES
    pitch = p_rows + PITCH_PAD
    i = pl.program_id(1)
    nt = pl.num_programs(1)

    def u_row(c, idx):
        if idx < 0:
            return up_ref[0, c, HALO + idx]
        if idx >= p_rows:
            return ux_ref[0, c, idx - p_rows]
        return um_ref[0, c, idx]

    def chunk(c, carry):
        off = pl.multiple_of(c * lc, lc)
        lanes = pl.ds(off, lc)
        b = dwb_ref[:, lanes]
        for r in range(p_rows):
            acc = jnp.zeros((STREAMS, lc), F32)
            for k in range(CONF_KERNEL):
                acc = acc + u_row(c, r + k - CONF_PAD).astype(F32) * w16_ref[c, k].astype(F32)
            c_scr[r * STREAMS:(r + 1) * STREAMS, lanes] = acc + b

        zero_row = jnp.zeros((1, lc), F32)

        @pl.when(i == 0)
        def _():
            for r in range(CONF_PAD):
                fix = jnp.zeros((STREAMS, lc), F32)
                for k in range(CONF_PAD - r):
                    idx = r + k - CONF_PAD
                    prev_stream = wrap_ref[0, c, HALO + idx].astype(F32)
                    true_row = jnp.concatenate([zero_row, prev_stream[0:STREAMS - 1]], axis=0)
                    fix = fix + (true_row - up_ref[0, c, HALO + idx].astype(F32)) * w16_ref[c, k].astype(F32)
                c_scr[r * STREAMS:(r + 1) * STREAMS, lanes] = c_scr[r * STREAMS:(r + 1) * STREAMS, lanes] + fix

        @pl.when(i == nt - 1)
        def _():
            for r in range(p_rows - CONF_PAD, p_rows):
                fix = jnp.zeros((STREAMS, lc), F32)
                for k in range(p_rows - r + CONF_PAD, CONF_KERNEL):
                    idx = r + k - CONF_PAD
                    next_stream = wrap_ref[0, c, idx - p_rows].astype(F32)
                    true_row = jnp.concatenate([next_stream[1:STREAMS], zero_row], axis=0)
                    fix = fix + (true_row - ux_ref[0, c, idx - p_rows].astype(F32)) * w16_ref[c, k].astype(F32)
                c_scr[r * STREAMS:(r + 1) * STREAMS, lanes] = c_scr[r * STREAMS:(r + 1) * STREAMS, lanes] + fix

        return carry

    lax.fori_loop(0, D_INNER // lc, chunk, 0)

    cv = c_scr[...]
    mu = jnp.mean(cv, axis=-1, keepdims=True)
    xc = cv - mu
    var = jnp.mean(xc * xc, axis=-1, keepdims=True)
    y = xc * lax.rsqrt(var + EPS) * lnw_ref[...] + lnb_ref[...]
    gate = (_silu(y) * _silu(z_ref[0].astype(F32))).astype(BF16)
    res_scr[...] = _dot(gate, wout_ref[...])

    def permute_out(r, carry):
        row0 = pl.multiple_of(r * STREAMS, STREAMS)
        for s in range(SLABS):
            slab_scr[s, pl.ds(r, STREAMS, stride=pitch), :] = res_scr[pl.ds(row0, STREAMS),
                                                                     s * LANES:(s + 1) * LANES]
        return carry

    lax.fori_loop(0, p_rows, permute_out, 0)
    for j in range(STREAMS):
        for s in range(SLABS):
            lanes = slice(s * LANES, (s + 1) * LANES)
            o_ref[0, j, :, lanes] = h_ref[0, j, :, lanes] + slab_scr[s, j * pitch:j * pitch + p_rows, :]


def _resident(shape):
    nd = len(shape)
    return pl.BlockSpec(shape, lambda b, i: (0,) * nd, pipeline_mode=pl.Buffered(1))


def _conformer_layer(h, nw, w_in, dw_w, dw_b, ln_w, ln_b, w_out, *, p_rows=32, lc_in=512):
    bsz, s, d = h.shape
    e = D_INNER
    sl = s // STREAMS
    assert s % STREAMS == 0 and sl % p_rows == 0 and p_rows % HALO == 0 and p_rows >= CONF_PAD
    nt = sl // p_rows
    ts = p_rows * STREAMS
    nch = e // LANES
    h4 = h.reshape(bsz, STREAMS, sl, d)
    params = pltpu.CompilerParams(dimension_semantics=("arbitrary", "arbitrary"), vmem_limit_bytes=VMEM_LIMIT)
    slab = pltpu.VMEM((SLABS, STREAMS * (p_rows + PITCH_PAD), LANES), F32)
    h_spec = pl.BlockSpec((1, STREAMS, p_rows, d), lambda b, i: (b, 0, i, 0))

    u, z = pl.pallas_call(
        functools.partial(_conf_in_kernel, p_rows=p_rows, lc=lc_in),
        grid=(bsz, nt),
        in_specs=[h_spec, _resident((1, d)), _resident((d, 3 * e))],
        out_specs=[pl.BlockSpec((1, nch, p_rows, STREAMS, LANES), lambda b, i: (b, 0, i, 0, 0)),
                   pl.BlockSpec((1, ts, e), lambda b, i: (b, i, 0))],
        out_shape=[jax.ShapeDtypeStruct((bsz, nch, sl, STREAMS, LANES), BF16),
                   jax.ShapeDtypeStruct((bsz, sl * STREAMS, e), BF16)],
        scratch_shapes=[slab, pltpu.VMEM((ts, d), BF16)],
        compiler_params=params,
        name="conformer_in",
    )(h4, nw.reshape(1, d), w_in.astype(BF16))

    per = p_rows // HALO
    last_halo = sl // HALO - 1
    w16 = jnp.broadcast_to(dw_w.astype(BF16).reshape(CONF_KERNEL, 1, nch, LANES),
                           (CONF_KERNEL, STREAMS, nch, LANES)).transpose(2, 0, 1, 3)
    halo_shape = (1, nch, HALO, STREAMS, LANES)
    out4 = pl.pallas_call(
        functools.partial(_conf_out_kernel, p_rows=p_rows),
        grid=(bsz, nt),
        in_specs=[
            pl.BlockSpec(halo_shape, lambda b, i: (b, 0, jnp.maximum(i * per - 1, 0), 0, 0)),
            pl.BlockSpec((1, nch, p_rows, STREAMS, LANES), lambda b, i: (b, 0, i, 0, 0)),
            pl.BlockSpec(halo_shape, lambda b, i: (b, 0, jnp.minimum((i + 1) * per, last_halo), 0, 0)),
            pl.BlockSpec(halo_shape, lambda b, i: (b, 0, jnp.where(i == nt - 1, 0, last_halo), 0, 0)),
            pl.BlockSpec((1, ts, e), lambda b, i: (b, i, 0)),
            h_spec,
            _resident((nch, CONF_KERNEL, STREAMS, LANES)),
            _resident((1, e)),
            _resident((1, e)),
            _resident((1, e)),
            _resident((e, d)),
        ],
        out_specs=h_spec,
        out_shape=jax.ShapeDtypeStruct((bsz, STREAMS, sl, d), F32),
        scratch_shapes=[pltpu.VMEM((ts, e), F32), pltpu.VMEM((ts, d), F32), slab],
        compiler_params=params,
        name="conformer_out",
    )(u, u, u, u, z, h4, w16, dw_b.reshape(1, e), ln_w.reshape(1, e), ln_b.reshape(1, e), w_out.astype(BF16))
    return out4.reshape(bsz, s, d)


def _halo_maps(ts, s):
    per = ts // HALO
    last = s // HALO - 1

    def prev_map(b, i):
        return (b, jnp.maximum(i * per - 1, 0), 0)

    def next_map(b, i):
        return (b, jnp.minimum((i + 1) * per, last), 0)

    return prev_map, next_map


def _ssd_in_kernel(hp_ref, hm_ref, hx_ref, nw_ref, wzx_ref, wdt_ref, cw_ref, cb_ref, dtb_ref,
                   z_ref, xbc_ref, dt_ref, u_scr, *, ts, lc):
    hn_all, valid = _normed_tile(hp_ref, hm_ref, hx_ref, nw_ref, ts)
    hn_m = hn_all[HALO:HALO + ts]
    n_chunks = SSD_CONV_DIM // lc

    def project(c, buf):
        off_w = pl.multiple_of(D_INNER + c * lc, lc)
        p = jnp.where(valid, _dot(hn_all, wzx_ref[:, pl.ds(off_w, lc)]), 0.0)
        for s in range(lc // LANES):
            u_scr[buf, s] = p[:, s * LANES:(s + 1) * LANES]

    def conv(c, buf):
        off = pl.multiple_of(c * lc, lc)
        w = cw_ref[:, pl.ds(off, lc)]
        b = cb_ref[:, pl.ds(off, lc)]

        def store(r0, acc):
            xbc_ref[0, r0:r0 + CONV_ROWS, pl.ds(off, lc)] = _silu(acc).astype(BF16)

        _dwconv_slabs(u_scr.at[buf], w, b, SSD_CONV, ts, store)

    project(0, 0)

    def pair(cp, carry):
        c0 = 2 * cp
        project(c0 + 1, 1)
        conv(c0, 0)
        project(c0 + 2, 0)
        conv(c0 + 1, 1)
        return carry

    lax.fori_loop(0, n_chunks // 2 - 1, pair, 0)
    project(n_chunks - 1, 1)
    conv(n_chunks - 2, 0)
    z_ref[0] = _dot(hn_m, wzx_ref[:, 0:D_INNER]).astype(BF16)
    conv(n_chunks - 1, 1)
    dt_ref[0] = _softplus(_dot(hn_m, wdt_ref[...]) + dtb_ref[...])


def _ssd_in_layer(h, nw, w_zx, w_dt, conv_w, conv_b, dt_bias, *, ts=512, lc=512):
    bsz, s, d = h.shape
    e, cd = D_INNER, SSD_CONV_DIM
    assert s % ts == 0 and ts % CONV_ROWS == 0 and ts % HALO == 0 and (cd // lc) % 2 == 0
    prev_map, next_map = _halo_maps(ts, s)
    kern = functools.partial(_ssd_in_kernel, ts=ts, lc=lc)
    tile = lambda b, i: (b, i, 0)
    return pl.pallas_call(
        kern,
        grid=(bsz, s // ts),
        in_specs=[
            pl.BlockSpec((1, HALO, d), prev_map),
            pl.BlockSpec((1, ts, d), tile),
            pl.BlockSpec((1, HALO, d), next_map),
            _resident((1, d)),
            _resident((d, e + cd)),
            _resident((d, 2 * LANES)),
            _resident((SSD_CONV, cd)),
            _resident((1, cd)),
            _resident((1, 2 * LANES)),
        ],
        out_specs=[
            pl.BlockSpec((1, ts, e), tile),
            pl.BlockSpec((1, ts, cd), tile),
            pl.BlockSpec((1, ts, 2 * LANES), tile),
        ],
        out_shape=[
            jax.ShapeDtypeStruct((bsz, s, e), BF16),
            jax.ShapeDtypeStruct((bsz, s, cd), BF16),
            jax.ShapeDtypeStruct((bsz, s, 2 * LANES), F32),
        ],
        scratch_shapes=[pltpu.VMEM((2, lc // LANES, ts + 2 * HALO, LANES), F32)],
        compiler_params=pltpu.CompilerParams(
            dimension_semantics=("arbitrary", "arbitrary"), vmem_limit_bytes=VMEM_LIMIT),
        name="ssd_in_proj",
    )(h, h, h, nw.reshape(1, d), w_zx, w_dt, conv_w, conv_b.reshape(1, cd), dt_bias)


MASKED = -1e30


def _split2_cat(w):
    hi = w.astype(BF16)
    lo = (w - hi.astype(F32)).astype(BF16)
    return jnp.concatenate([hi, lo], axis=1)


def _scan_prepare(dt, a_neg, rev):
    L = CHUNK
    a = dt * a_neg
    r = lax.broadcasted_iota(jnp.int32, (L, L), 0)
    cdx = lax.broadcasted_iota(jnp.int32, (L, L), 1)
    tri = (cdx >= r) if rev else (cdx <= r)
    tri_b = jnp.where(tri, 1.0, 0.0).astype(BF16)
    a1, a2, a3 = _split3(a)
    acs = _dot(tri_b, a1) + _dot(tri_b, a2) + _dot(tri_b, a3)
    src_t = (acs - jnp.log(dt)).T
    last = acs[0:1, :] if rev else acs[L - 1:L, :]
    w3s = _split2_cat(jnp.exp(acs))
    w2s = _split2_cat(dt * jnp.exp(last - acs))
    return tri, acs, src_t, w3s, w2s


def _scan_chunks(xs, bms, cms, dts, a_neg, e2_ref, state_scr, emit, rev):
    L = CHUNK
    nb = len(xs)
    prep = [_scan_prepare(dt, a_neg, rev) for dt in dts]
    lane = lax.broadcasted_iota(jnp.int32, (L, LANES), 1)
    lo_half = lane < HEADDIM
    wcat = jnp.concatenate([w for p in prep for w in (p[3], p[4])], axis=0)

    for g in range(SSD_GROUPS):
        gsl = slice(g * GROUP_WIDTH, (g + 1) * GROUP_WIDTH)
        wexp = _dot(wcat, e2_ref[:, gsl])
        for bi in range(nb):
            tri, acs, src_t = prep[bi][0], prep[bi][1], prep[bi][2]
            x = xs[bi]
            bg = bms[bi][:, g * D_STATE:(g + 1) * D_STATE]
            cg = cms[bi][:, g * D_STATE:(g + 1) * D_STATE]
            cb = lax.dot_general(cg, bg, (((1,), (1,)), ((), ())), preferred_element_type=F32)
            yds = []
            for qq in range(GROUP_WIDTH // LANES):
                q = g * (GROUP_WIDTH // LANES) + qq
                ms = []
                for hh in (2 * q, 2 * q + 1):
                    d = acs[:, hh:hh + 1] - src_t[hh:hh + 1, :]
                    ms.append((cb * jnp.exp(jnp.where(tri, d, MASKED))).astype(BF16))
                mcat = jnp.concatenate(ms, axis=1)
                xq = x[:, q * LANES:(q + 1) * LANES]
                zero = jnp.zeros_like(xq)
                xbd = jnp.concatenate([jnp.where(lo_half, xq, zero), jnp.where(lo_half, zero, xq)], axis=0)
                yds.append(_dot(mcat, xbd))
            w3e = wexp[bi * 2 * L:bi * 2 * L + L]
            w2e = wexp[bi * 2 * L + L:(bi + 1) * 2 * L]
            st = state_scr[bi, :, gsl]
            emit(bi, gsl, jnp.concatenate(yds, axis=1) + _dot(cg, st.astype(BF16)) * w3e)
            xsc = (x[:, gsl].astype(F32) * w2e).astype(BF16)
            bg_t = bg.astype(F32).T.astype(BF16)
            dec = w3e[0:1, :] if rev else w3e[L - 1:L, :]
            state_scr[bi, :, gsl] = st * dec + _dot(bg_t, xsc)


def _ssd_fwd_kernel(x_ref, b_ref, c_ref, dt_ref, alog_ref, e2_ref, y_ref, state_scr, *, n_sub):
    @pl.when(pl.program_id(0) == 0)
    def _():
        state_scr[...] = jnp.zeros_like(state_scr)

    a_neg = -jnp.exp(alog_ref[...])

    def body(ci, carry):
        rows = pl.ds(pl.multiple_of(ci * CHUNK, CHUNK), CHUNK)
        nb = x_ref.shape[0]

        def emit(bi, lanes, y):
            y_ref[bi, rows, lanes] = y.astype(BF16)

        _scan_chunks([x_ref[bi, rows, :] for bi in range(nb)], [b_ref[bi, rows, :] for bi in range(nb)],
                     [c_ref[bi, rows, :] for bi in range(nb)], [dt_ref[bi, rows, :] for bi in range(nb)],
                     a_neg, e2_ref, state_scr, emit, False)
        return carry

    lax.fori_loop(0, n_sub, body, 0)


def _ssd_bwd_out_kernel(x_ref, b_ref, c_ref, dt_ref, alog_ref, e2_ref, yf_ref, z_ref, h_ref, dexp_ref,
                        gnw_ref, wout_ref, fnw_ref, o_ref, state_scr, y_scr, *, n_sub, final):
    @pl.when(pl.program_id(0) == 0)
    def _():
        state_scr[...] = jnp.zeros_like(state_scr)

    a_neg = -jnp.exp(alog_ref[...])
    bsz = x_ref.shape[0]
    ts = n_sub * CHUNK

    def body(k, carry):
        row0 = pl.multiple_of((n_sub - 1 - k) * CHUNK, CHUNK)
        rows = pl.ds(row0, CHUNK)
        xs = [x_ref[bi, rows, :] for bi in range(bsz)]
        sumsq = [jnp.zeros((CHUNK, 1), F32) for _ in range(bsz)]

        def emit(bi, lanes, y_bwd):
            y = y_bwd + yf_ref[bi, rows, lanes].astype(F32) + xs[bi][:, lanes].astype(F32) * dexp_ref[:, lanes]
            gated = y * _silu(z_ref[bi, rows, lanes].astype(F32))
            y_scr[bi, :, lanes] = gated
            sumsq[bi] = sumsq[bi] + jnp.sum(gated * gated, axis=-1, keepdims=True)

        _scan_chunks(xs, [b_ref[bi, rows, :] for bi in range(bsz)], [c_ref[bi, rows, :] for bi in range(bsz)],
                     [dt_ref[bi, rows, :] for bi in range(bsz)], a_neg, e2_ref, state_scr, emit, True)
        gs = []
        for bi in range(bsz):
            inv = lax.rsqrt(sumsq[bi] * (1.0 / D_INNER) + EPS)
            gs.append((y_scr[bi] * inv * gnw_ref[...]).astype(BF16))
        res = _dot(jnp.concatenate(gs, axis=0), wout_ref[...])
        for bi in range(bsz):
            out = h_ref[bi, rows, :] + res[bi * CHUNK:(bi + 1) * CHUNK]
            if final:
                ms = jnp.mean(out * out, axis=-1, keepdims=True)
                out = out * lax.rsqrt(ms + EPS) * fnw_ref[...]
            o_ref[bi, rows, :] = out
        return carry

    lax.fori_loop(0, n_sub, body, 0)


def _ssd_scans(h, xbc, z, dt, alog_pad, e2, dexp, gnw, w_out, fnw, *, final, n_sub=2):
    bsz, s, d = h.shape
    e = D_INNER
    ts = n_sub * CHUNK
    assert s % ts == 0
    nblk = s // ts
    nb = D_INNER // (SSD_GROUPS * D_STATE)
    gw = SSD_GROUPS * D_STATE
    params = pltpu.CompilerParams(dimension_semantics=("arbitrary",), vmem_limit_bytes=VMEM_LIMIT)

    def resident(shape):
        nd = len(shape)
        return pl.BlockSpec(shape, lambda j: (0,) * nd, pipeline_mode=pl.Buffered(1))

    def scan_specs(blk, d_idx):
        return [
            pl.BlockSpec((bsz, ts, e), lambda j: (0, blk(j), 0)),
            pl.BlockSpec((bsz, ts, gw), lambda j: (0, blk(j), nb)),
            pl.BlockSpec((bsz, ts, gw), lambda j: (0, blk(j), nb + 1)),
            pl.BlockSpec((bsz, ts, LANES), lambda j: (0, blk(j), d_idx)),
            pl.BlockSpec((1, LANES), lambda j: (0, d_idx)),
            resident((2 * LANES, e)),
        ]

    state = pltpu.VMEM((bsz, D_STATE, e), F32)
    fwd = lambda j: j
    yf = pl.pallas_call(
        functools.partial(_ssd_fwd_kernel, n_sub=n_sub),
        grid=(nblk,),
        in_specs=scan_specs(fwd, 0),
        out_specs=pl.BlockSpec((bsz, ts, e), lambda j: (0, j, 0)),
        out_shape=jax.ShapeDtypeStruct((bsz, s, e), BF16),
        scratch_shapes=[state],
        compiler_params=params,
        name="ssd_scan_fwd",
    )(xbc, xbc, xbc, dt, alog_pad, e2)

    bwd = lambda j: nblk - 1 - j
    tile_b = lambda j: (0, bwd(j), 0)
    return pl.pallas_call(
        functools.partial(_ssd_bwd_out_kernel, n_sub=n_sub, final=final),
        grid=(nblk,),
        in_specs=scan_specs(bwd, 1) + [
            pl.BlockSpec((bsz, ts, e), tile_b),
            pl.BlockSpec((bsz, ts, e), tile_b),
            pl.BlockSpec((bsz, ts, d), tile_b),
            resident((1, e)),
            resident((1, e)),
            resident((e, d)),
            resident((1, d)),
        ],
        out_specs=pl.BlockSpec((bsz, ts, d), tile_b),
        out_shape=jax.ShapeDtypeStruct((bsz, s, d), F32),
        scratch_shapes=[state, pltpu.VMEM((bsz, CHUNK, e), F32)],
        compiler_params=params,
        name="ssd_scan_bwd_out",
    )(xbc, xbc, xbc, dt, alog_pad, e2, yf, z, h, dexp, gnw.reshape(1, e), w_out.astype(BF16), fnw.reshape(1, d))


def _pad_lanes(v):
    return jnp.pad(v, ((0, 0), (0, LANES - SSD_HEADS))).reshape(1, 2 * LANES)


def _expand_matrix():
    r = jnp.arange(2 * LANES)[:, None] % LANES
    c = jnp.arange(D_INNER)[None, :] // HEADDIM
    return (r == c).astype(BF16)


def _ssd_layer(h, nw, w_in, conv_w, conv_b, dt_bias, a_log, d_skip, gnw, w_out, fnw, *, final):
    d = h.shape[-1]
    e, cd = D_INNER, SSD_CONV_DIM
    w_zx = w_in[:, :e + cd].astype(BF16)
    w_dt = jnp.pad(w_in[:, e + cd:].reshape(d, 2, SSD_HEADS),
                   ((0, 0), (0, 0), (0, LANES - SSD_HEADS))).reshape(d, 2 * LANES).astype(BF16)
    z, xbc, dt = _ssd_in_layer(h, nw, w_zx, w_dt, conv_w, conv_b, _pad_lanes(dt_bias))
    dexp = jnp.repeat(d_skip, HEADDIM).reshape(1, e)
    return _ssd_scans(h, xbc, z, dt, _pad_lanes(a_log), _expand_matrix(), dexp, gnw, w_out, fnw, final=final)


def kernel(x, norm_w, final_norm_w, cm_w_in, cm_dw_w, cm_dw_b, cm_ln_w, cm_ln_b, cm_w_out, ssd_w_in,
           ssd_conv_w, ssd_conv_b, ssd_dt_bias, ssd_A_log, ssd_D, ssd_norm_w, ssd_w_out):
    depth = norm_w.shape[0]
    assert depth % 2 == 0
    h = x
    for i in range(depth):
        j = i // 2
        if i % 2 == 0:
            h = _conformer_layer(h, norm_w[i], cm_w_in[j], cm_dw_w[j], cm_dw_b[j], cm_ln_w[j],
                                 cm_ln_b[j], cm_w_out[j])
        else:
            h = _ssd_layer(h, norm_w[i], ssd_w_in[j], ssd_conv_w[j], ssd_conv_b[j], ssd_dt_bias[j],
                           ssd_A_log[j], ssd_D[j], ssd_norm_w[j], ssd_w_out[j], final_norm_w,
                           final=(i == depth - 1))
    return h
```

```python
import functools

import jax
import jax.numpy as jnp
from jax import lax
from jax.experimental import pallas as pl
from jax.experimental.pallas import tpu as pltpu

F32 = jnp.float32
BF16 = jnp.bfloat16

EPS = 1e-5
D_MODEL = 1024
D_INNER = 2048
CONF_KERNEL = 31
HEADDIM = 64
SSD_HEADS = 32
SSD_GROUPS = 4
D_STATE = 128
SSD_CONV = 5
CHUNK = 128
SSD_CONV_DIM = D_INNER + 2 * SSD_GROUPS * D_STATE
GROUP_WIDTH = D_INNER // SSD_GROUPS
LANES = 128
HALO = 16
CONV_ROWS = 32
VMEM_LIMIT = 56 * 1024 * 1024


def _sigmoid(x):
    return 1.0 / (1.0 + jnp.exp(-x))


def _silu(x):
    return x * _sigmoid(x)


def _softplus(x):
    return jnp.maximum(x, 0.0) + jnp.log1p(jnp.exp(-jnp.abs(x)))


def _rms_bf16(x, w):
    ms = jnp.mean(x * x, axis=-1, keepdims=True)
    return (x * lax.rsqrt(ms + EPS) * w).astype(BF16)


def _dot(a, b):
    return jnp.dot(a, b, preferred_element_type=F32)


def _split3(a):
    a1 = a.astype(BF16)
    r1 = a - a1.astype(F32)
    a2 = r1.astype(BF16)
    a3 = (r1 - a2.astype(F32)).astype(BF16)
    return a1, a2, a3


def _normed_tile(hp_ref, hm_ref, hx_ref, nw_ref, ts):
    i = pl.program_id(1)
    nt = pl.num_programs(1)
    hall = jnp.concatenate([hp_ref[0], hm_ref[0], hx_ref[0]], axis=0)
    hn_all = _rms_bf16(hall, nw_ref[...])
    rows = lax.broadcasted_iota(jnp.int32, (ts + 2 * HALO, 1), 0)
    lo = jnp.where(i == 0, HALO, 0)
    hi = jnp.where(i == nt - 1, HALO + ts, ts + 2 * HALO)
    valid = (rows >= lo) & (rows < hi)
    return hn_all, valid


def _dwconv_slabs(u_ref, w, b, width, ts, store):
    pad = (width - 1) // 2
    for rb in range(ts // CONV_ROWS):
        r0 = rb * CONV_ROWS
        accs = []
        for s in range(u_ref.shape[0]):
            lanes = slice(s * LANES, (s + 1) * LANES)
            acc = jnp.broadcast_to(b[:, lanes], (CONV_ROWS, LANES))
            for k in range(width):
                acc = acc + w[k:k + 1, lanes] * u_ref[s, pl.ds(r0 + HALO - pad + k, CONV_ROWS), :]
            accs.append(acc)
        store(r0, jnp.concatenate(accs, axis=1))


STREAMS = 16
PITCH_PAD = 8
SLABS = D_MODEL // LANES
CONF_PAD = (CONF_KERNEL - 1) // 2


def _conf_in_kernel(h_ref, nw_ref, win_ref, u_ref, z_ref, slab_scr, hnp_scr, *, p_rows, lc):
    pitch = p_rows + PITCH_PAD
    nw = nw_ref[...]
    for j in range(STREAMS):
        hj = h_ref[0, j]
        hn = hj * lax.rsqrt(jnp.mean(hj * hj, axis=-1, keepdims=True) + EPS) * nw
        for s in range(SLABS):
            slab_scr[s, j * pitch:j * pitch + p_rows, :] = hn[:, s * LANES:(s + 1) * LANES]

    def permute_in(r, carry):
        row0 = pl.multiple_of(r * STREAMS, STREAMS)
        for s in range(SLABS):
            piece = slab_scr[s, pl.ds(r, STREAMS, stride=pitch), :]
            hnp_scr[pl.ds(row0, STREAMS), s * LANES:(s + 1) * LANES] = piece.astype(BF16)
        return carry

    lax.fori_loop(0, p_rows, permute_in, 0)

    def chunk(c, carry):
        off_v = pl.multiple_of(c * lc, lc)
        off_g = pl.multiple_of(D_INNER + c * lc, lc)
        off_z = pl.multiple_of(2 * D_INNER + c * lc, lc)
        lhs = hnp_scr[...]
        u = _dot(lhs, win_ref[:, pl.ds(off_v, lc)]) * _sigmoid(_dot(lhs, win_ref[:, pl.ds(off_g, lc)]))
        for q in range(lc // LANES):
            u_ref[0, c * (lc // LANES) + q] = u[:, q * LANES:(q + 1) * LANES].reshape(
                p_rows, STREAMS, LANES).astype(BF16)
        z_ref[0, :, pl.ds(off_v, lc)] = _dot(lhs, win_ref[:, pl.ds(off_z, lc)]).astype(BF16)
        return carry

    lax.fori_loop(0, D_INNER // lc, chunk, 0)


def _conf_out_kernel(up_ref, um_ref, ux_ref, wrap_ref, z_ref, h_ref, w16_ref, dwb_ref, lnw_ref, lnb_ref,
                     wout_ref, o_ref, c_scr, res_scr, slab_scr, *, p_rows):
    lc = LANES
    pitch = p_rows + PITCH_PAD
    i = pl.program_id(1)
    nt = pl.num_programs(1)

    def u_row(c, idx):
        if idx < 0:
            return up_ref[0, c, HALO + idx]
        if idx >= p_rows:
            return ux_ref[0, c, idx - p_rows]
        return um_ref[0, c, idx]

    def chunk(c, carry):
        off = pl.multiple_of(c * lc, lc)
        lanes = pl.ds(off, lc)
        b = dwb_ref[:, lanes]
        for r in range(p_rows):
            acc = jnp.zeros((STREAMS, lc), F32)
            for k in range(CONF_KERNEL):
                acc = acc + u_row(c, r + k - CONF_PAD).astype(F32) * w16_ref[c, k].astype(F32)
            c_scr[r * STREAMS:(r + 1) * STREAMS, lanes] = acc + b

        zero_row = jnp.zeros((1, lc), F32)

        @pl.when(i == 0)
        def _():
            for r in range(CONF_PAD):
                fix = jnp.zeros((STREAMS, lc), F32)
                for k in range(CONF_PAD - r):
                    idx = r + k - CONF_PAD
                    prev_stream = wrap_ref[0, c, HALO + idx].astype(F32)
                    true_row = jnp.concatenate([zero_row, prev_stream[0:STREAMS - 1]], axis=0)
                    fix = fix + (true_row - up_ref[0, c, HALO + idx].astype(F32)) * w16_ref[c, k].astype(F32)
                c_scr[r * STREAMS:(r + 1) * STREAMS, lanes] = c_scr[r * STREAMS:(r + 1) * STREAMS, lanes] + fix

        @pl.when(i == nt - 1)
        def _():
            for r in range(p_rows - CONF_PAD, p_rows):
                fix = jnp.zeros((STREAMS, lc), F32)
                for k in range(p_rows - r + CONF_PAD, CONF_KERNEL):
                    idx = r + k - CONF_PAD
                    next_stream = wrap_ref[0, c, idx - p_rows].astype(F32)
                    true_row = jnp.concatenate([next_stream[1:STREAMS], zero_row], axis=0)
                    fix = fix + (true_row - ux_ref[0, c, idx - p_rows].astype(F32)) * w16_ref[c, k].astype(F32)
                c_scr[r * STREAMS:(r + 1) * STREAMS, lanes] = c_scr[r * STREAMS:(r + 1) * STREAMS, lanes] + fix

        return carry

    lax.fori_loop(0, D_INNER // lc, chunk, 0)

    cv = c_scr[...]
    mu = jnp.mean(cv, axis=-1, keepdims=True)
    xc = cv - mu
    var = jnp.mean(xc * xc, axis=-1, keepdims=True)
    y = xc * lax.rsqrt(var + EPS) * lnw_ref[...] + lnb_ref[...]
    gate = (_silu(y) * _silu(z_ref[0].astype(F32))).astype(BF16)
    res_scr[...] = _dot(gate, wout_ref[...])

    def permute_out(r, carry):
        row0 = pl.multiple_of(r * STREAMS, STREAMS)
        for s in range(SLABS):
            slab_scr[s, pl.ds(r, STREAMS, stride=pitch), :] = res_scr[pl.ds(row0, STREAMS),
                                                                     s * LANES:(s + 1) * LANES]
        return carry

    lax.fori_loop(0, p_rows, permute_out, 0)
    for j in range(STREAMS):
        for s in range(SLABS):
            lanes = slice(s * LANES, (s + 1) * LANES)
            o_ref[0, j, :, lanes] = h_ref[0, j, :, lanes] + slab_scr[s, j * pitch:j * pitch + p_rows, :]


def _resident(shape):
    nd = len(shape)
    return pl.BlockSpec(shape, lambda b, i: (0,) * nd, pipeline_mode=pl.Buffered(1))


def _conformer_layer(h, nw, w_in, dw_w, dw_b, ln_w, ln_b, w_out, *, p_rows=32, p_rows_in=64, lc_in=512):
    bsz, s, d = h.shape
    e = D_INNER
    sl = s // STREAMS
    assert s % STREAMS == 0 and sl % p_rows == 0 and p_rows % HALO == 0 and p_rows >= CONF_PAD
    assert sl % p_rows_in == 0 and p_rows_in % 8 == 0
    nt = sl // p_rows
    ts = p_rows * STREAMS
    nch = e // LANES
    h4 = h.reshape(bsz, STREAMS, sl, d)
    params = pltpu.CompilerParams(dimension_semantics=("arbitrary", "arbitrary"), vmem_limit_bytes=VMEM_LIMIT)
    slab = pltpu.VMEM((SLABS, STREAMS * (p_rows + PITCH_PAD), LANES), F32)
    h_spec = pl.BlockSpec((1, STREAMS, p_rows, d), lambda b, i: (b, 0, i, 0))

    ts_in = p_rows_in * STREAMS
    u, z = pl.pallas_call(
        functools.partial(_conf_in_kernel, p_rows=p_rows_in, lc=lc_in),
        grid=(bsz, sl // p_rows_in),
        in_specs=[pl.BlockSpec((1, STREAMS, p_rows_in, d), lambda b, i: (b, 0, i, 0)), _resident((1, d)),
                  _resident((d, 3 * e))],
        out_specs=[pl.BlockSpec((1, nch, p_rows_in, STREAMS, LANES), lambda b, i: (b, 0, i, 0, 0)),
                   pl.BlockSpec((1, ts_in, e), lambda b, i: (b, i, 0))],
        out_shape=[jax.ShapeDtypeStruct((bsz, nch, sl, STREAMS, LANES), BF16),
                   jax.ShapeDtypeStruct((bsz, sl * STREAMS, e), BF16)],
        scratch_shapes=[pltpu.VMEM((SLABS, STREAMS * (p_rows_in + PITCH_PAD), LANES), F32),
                        pltpu.VMEM((ts_in, d), BF16)],
        compiler_params=params,
        name="conformer_in",
    )(h4, nw.reshape(1, d), w_in.astype(BF16))

    per = p_rows // HALO
    last_halo = sl // HALO - 1
    w16 = jnp.broadcast_to(dw_w.astype(BF16).reshape(CONF_KERNEL, 1, nch, LANES),
                           (CONF_KERNEL, STREAMS, nch, LANES)).transpose(2, 0, 1, 3)
    halo_shape = (1, nch, HALO, STREAMS, LANES)
    out4 = pl.pallas_call(
        functools.partial(_conf_out_kernel, p_rows=p_rows),
        grid=(bsz, nt),
        in_specs=[
            pl.BlockSpec(halo_shape, lambda b, i: (b, 0, jnp.maximum(i * per - 1, 0), 0, 0)),
            pl.BlockSpec((1, nch, p_rows, STREAMS, LANES), lambda b, i: (b, 0, i, 0, 0)),
            pl.BlockSpec(halo_shape, lambda b, i: (b, 0, jnp.minimum((i + 1) * per, last_halo), 0, 0)),
            pl.BlockSpec(halo_shape, lambda b, i: (b, 0, jnp.where(i == nt - 1, 0, last_halo), 0, 0)),
            pl.BlockSpec((1, ts, e), lambda b, i: (b, i, 0)),
            h_spec,
            _resident((nch, CONF_KERNEL, STREAMS, LANES)),
            _resident((1, e)),
            _resident((1, e)),
            _resident((1, e)),
            _resident((e, d)),
        ],
        out_specs=h_spec,
        out_shape=jax.ShapeDtypeStruct((bsz, STREAMS, sl, d), F32),
        scratch_shapes=[pltpu.VMEM((ts, e), F32), pltpu.VMEM((ts, d), F32), slab],
        compiler_params=params,
        name="conformer_out",
    )(u, u, u, u, z, h4, w16, dw_b.reshape(1, e), ln_w.reshape(1, e), ln_b.reshape(1, e), w_out.astype(BF16))
    return out4.reshape(bsz, s, d)


def _halo_maps(ts, s):
    per = ts // HALO
    last = s // HALO - 1

    def prev_map(b, i):
        return (b, jnp.maximum(i * per - 1, 0), 0)

    def next_map(b, i):
        return (b, jnp.minimum((i + 1) * per, last), 0)

    return prev_map, next_map


def _ssd_in_kernel(hp_ref, hm_ref, hx_ref, nw_ref, wzx_ref, wdt_ref, cw_ref, cb_ref, dtb_ref,
                   z_ref, xbc_ref, dt_ref, u_scr, *, ts, lc):
    hn_all, valid = _normed_tile(hp_ref, hm_ref, hx_ref, nw_ref, ts)
    hn_m = hn_all[HALO:HALO + ts]
    n_chunks = SSD_CONV_DIM // lc

    def project(c, buf):
        off_w = pl.multiple_of(D_INNER + c * lc, lc)
        p = jnp.where(valid, _dot(hn_all, wzx_ref[:, pl.ds(off_w, lc)]), 0.0)
        for s in range(lc // LANES):
            u_scr[buf, s] = p[:, s * LANES:(s + 1) * LANES]

    def conv(c, buf):
        off = pl.multiple_of(c * lc, lc)
        w = cw_ref[:, pl.ds(off, lc)]
        b = cb_ref[:, pl.ds(off, lc)]

        def store(r0, acc):
            xbc_ref[0, r0:r0 + CONV_ROWS, pl.ds(off, lc)] = _silu(acc).astype(BF16)

        _dwconv_slabs(u_scr.at[buf], w, b, SSD_CONV, ts, store)

    project(0, 0)

    def pair(cp, carry):
        c0 = 2 * cp
        project(c0 + 1, 1)
        conv(c0, 0)
        project(c0 + 2, 0)
        conv(c0 + 1, 1)
        return carry

    lax.fori_loop(0, n_chunks // 2 - 1, pair, 0)
    project(n_chunks - 1, 1)
    conv(n_chunks - 2, 0)
    z_ref[0] = _dot(hn_m, wzx_ref[:, 0:D_INNER]).astype(BF16)
    conv(n_chunks - 1, 1)
    dt_ref[0] = _softplus(_dot(hn_m, wdt_ref[...]) + dtb_ref[...])


def _ssd_in_layer(h, nw, w_zx, w_dt, conv_w, conv_b, dt_bias, *, ts=512, lc=512):
    bsz, s, d = h.shape
    e, cd = D_INNER, SSD_CONV_DIM
    assert s % ts == 0 and ts % CONV_ROWS == 0 and ts % HALO == 0 and (cd // lc) % 2 == 0
    prev_map, next_map = _halo_maps(ts, s)
    kern = functools.partial(_ssd_in_kernel, ts=ts, lc=lc)
    tile = lambda b, i: (b, i, 0)
    return pl.pallas_call(
        kern,
        grid=(bsz, s // ts),
        in_specs=[
            pl.BlockSpec((1, HALO, d), prev_map),
            pl.BlockSpec((1, ts, d), tile),
            pl.BlockSpec((1, HALO, d), next_map),
            _resident((1, d)),
            _resident((d, e + cd)),
            _resident((d, 2 * LANES)),
            _resident((SSD_CONV, cd)),
            _resident((1, cd)),
            _resident((1, 2 * LANES)),
        ],
        out_specs=[
            pl.BlockSpec((1, ts, e), tile),
            pl.BlockSpec((1, ts, cd), tile),
            pl.BlockSpec((1, ts, 2 * LANES), tile),
        ],
        out_shape=[
            jax.ShapeDtypeStruct((bsz, s, e), BF16),
            jax.ShapeDtypeStruct((bsz, s, cd), BF16),
            jax.ShapeDtypeStruct((bsz, s, 2 * LANES), F32),
        ],
        scratch_shapes=[pltpu.VMEM((2, lc // LANES, ts + 2 * HALO, LANES), F32)],
        compiler_params=pltpu.CompilerParams(
            dimension_semantics=("arbitrary", "arbitrary"), vmem_limit_bytes=VMEM_LIMIT),
        name="ssd_in_proj",
    )(h, h, h, nw.reshape(1, d), w_zx, w_dt, conv_w, conv_b.reshape(1, cd), dt_bias)


MASKED = -1e30


def _split2_cat(w):
    hi = w.astype(BF16)
    lo = (w - hi.astype(F32)).astype(BF16)
    return jnp.concatenate([hi, lo], axis=1)


def _scan_prepare(dt, a_neg, rev):
    L = CHUNK
    a = dt * a_neg
    r = lax.broadcasted_iota(jnp.int32, (L, L), 0)
    cdx = lax.broadcasted_iota(jnp.int32, (L, L), 1)
    tri = (cdx >= r) if rev else (cdx <= r)
    tri_b = jnp.where(tri, 1.0, 0.0).astype(BF16)
    a1, a2, a3 = _split3(a)
    acs = _dot(tri_b, a1) + _dot(tri_b, a2) + _dot(tri_b, a3)
    src_t = (acs - jnp.log(dt)).T
    last = acs[0:1, :] if rev else acs[L - 1:L, :]
    w3s = _split2_cat(jnp.exp(acs))
    w2s = _split2_cat(dt * jnp.exp(last - acs))
    return tri, acs, src_t, w3s, w2s


def _scan_chunks(xs, bms, cms, dts, a_neg, e2_ref, state_scr, emit, rev):
    L = CHUNK
    nb = len(xs)
    prep = [_scan_prepare(dt, a_neg, rev) for dt in dts]
    lane = lax.broadcasted_iota(jnp.int32, (L, LANES), 1)
    lo_half = lane < HEADDIM
    wcat = jnp.concatenate([w for p in prep for w in (p[3], p[4])], axis=0)

    for g in range(SSD_GROUPS):
        gsl = slice(g * GROUP_WIDTH, (g + 1) * GROUP_WIDTH)
        wexp = _dot(wcat, e2_ref[:, gsl])
        for bi in range(nb):
            tri, acs, src_t = prep[bi][0], prep[bi][1], prep[bi][2]
            x = xs[bi]
            bg = bms[bi][:, g * D_STATE:(g + 1) * D_STATE]
            cg = cms[bi][:, g * D_STATE:(g + 1) * D_STATE]
            cb = lax.dot_general(cg, bg, (((1,), (1,)), ((), ())), preferred_element_type=F32)
            yds = []
            for qq in range(GROUP_WIDTH // LANES):
                q = g * (GROUP_WIDTH // LANES) + qq
                ms = []
                for hh in (2 * q, 2 * q + 1):
                    d = acs[:, hh:hh + 1] - src_t[hh:hh + 1, :]
                    ms.append((cb * jnp.exp(jnp.where(tri, d, MASKED))).astype(BF16))
                mcat = jnp.concatenate(ms, axis=1)
                xq = x[:, q * LANES:(q + 1) * LANES]
                zero = jnp.zeros_like(xq)
                xbd = jnp.concatenate([jnp.where(lo_half, xq, zero), jnp.where(lo_half, zero, xq)], axis=0)
                yds.append(_dot(mcat, xbd))
            w3e = wexp[bi * 2 * L:bi * 2 * L + L]
            w2e = wexp[bi * 2 * L + L:(bi + 1) * 2 * L]
            st = state_scr[bi, :, gsl]
            emit(bi, gsl, jnp.concatenate(yds, axis=1) + _dot(cg, st.astype(BF16)) * w3e)
            xsc = (x[:, gsl].astype(F32) * w2e).astype(BF16)
            bg_t = bg.astype(F32).T.astype(BF16)
            dec = w3e[0:1, :] if rev else w3e[L - 1:L, :]
            state_scr[bi, :, gsl] = st * dec + _dot(bg_t, xsc)


def _ssd_fwd_kernel(x_ref, b_ref, c_ref, dt_ref, alog_ref, e2_ref, y_ref, state_scr, *, n_sub):
    @pl.when(pl.program_id(0) == 0)
    def _():
        state_scr[...] = jnp.zeros_like(state_scr)

    a_neg = -jnp.exp(alog_ref[...])

    def body(ci, carry):
        rows = pl.ds(pl.multiple_of(ci * CHUNK, CHUNK), CHUNK)
        nb = x_ref.shape[0]

        def emit(bi, lanes, y):
            y_ref[bi, rows, lanes] = y.astype(BF16)

        _scan_chunks([x_ref[bi, rows, :] for bi in range(nb)], [b_ref[bi, rows, :] for bi in range(nb)],
                     [c_ref[bi, rows, :] for bi in range(nb)], [dt_ref[bi, rows, :] for bi in range(nb)],
                     a_neg, e2_ref, state_scr, emit, False)
        return carry

    lax.fori_loop(0, n_sub, body, 0)


def _ssd_bwd_out_kernel(x_ref, b_ref, c_ref, dt_ref, alog_ref, e2_ref, yf_ref, z_ref, h_ref, dexp_ref,
                        gnw_ref, wout_ref, fnw_ref, o_ref, state_scr, y_scr, *, n_sub, final):
    @pl.when(pl.program_id(0) == 0)
    def _():
        state_scr[...] = jnp.zeros_like(state_scr)

    a_neg = -jnp.exp(alog_ref[...])
    bsz = x_ref.shape[0]
    ts = n_sub * CHUNK

    def body(k, carry):
        row0 = pl.multiple_of((n_sub - 1 - k) * CHUNK, CHUNK)
        rows = pl.ds(row0, CHUNK)
        xs = [x_ref[bi, rows, :] for bi in range(bsz)]
        sumsq = [jnp.zeros((CHUNK, 1), F32) for _ in range(bsz)]

        def emit(bi, lanes, y_bwd):
            y = y_bwd + yf_ref[bi, rows, lanes].astype(F32) + xs[bi][:, lanes].astype(F32) * dexp_ref[:, lanes]
            gated = y * _silu(z_ref[bi, rows, lanes].astype(F32))
            y_scr[bi, :, lanes] = gated
            sumsq[bi] = sumsq[bi] + jnp.sum(gated * gated, axis=-1, keepdims=True)

        _scan_chunks(xs, [b_ref[bi, rows, :] for bi in range(bsz)], [c_ref[bi, rows, :] for bi in range(bsz)],
                     [dt_ref[bi, rows, :] for bi in range(bsz)], a_neg, e2_ref, state_scr, emit, True)
        gs = []
        for bi in range(bsz):
            inv = lax.rsqrt(sumsq[bi] * (1.0 / D_INNER) + EPS)
            gs.append((y_scr[bi] * inv * gnw_ref[...]).astype(BF16))
        res = _dot(jnp.concatenate(gs, axis=0), wout_ref[...])
        for bi in range(bsz):
            out = h_ref[bi, rows, :] + res[bi * CHUNK:(bi + 1) * CHUNK]
            if final:
                ms = jnp.mean(out * out, axis=-1, keepdims=True)
                out = out * lax.rsqrt(ms + EPS) * fnw_ref[...]
            o_ref[bi, rows, :] = out
        return carry

    lax.fori_loop(0, n_sub, body, 0)


def _ssd_scans(h, xbc, z, dt, alog_pad, e2, dexp, gnw, w_out, fnw, *, final, n_sub=2):
    bsz, s, d = h.shape
    e = D_INNER
    ts = n_sub * CHUNK
    assert s % ts == 0
    nblk = s // ts
    nb = D_INNER // (SSD_GROUPS * D_STATE)
    gw = SSD_GROUPS * D_STATE
    params = pltpu.CompilerParams(dimension_semantics=("arbitrary",), vmem_limit_bytes=VMEM_LIMIT)

    def resident(shape):
        nd = len(shape)
        return pl.BlockSpec(shape, lambda j: (0,) * nd, pipeline_mode=pl.Buffered(1))

    def scan_specs(blk, d_idx):
        return [
            pl.BlockSpec((bsz, ts, e), lambda j: (0, blk(j), 0)),
            pl.BlockSpec((bsz, ts, gw), lambda j: (0, blk(j), nb)),
            pl.BlockSpec((bsz, ts, gw), lambda j: (0, blk(j), nb + 1)),
            pl.BlockSpec((bsz, ts, LANES), lambda j: (0, blk(j), d_idx)),
            pl.BlockSpec((1, LANES), lambda j: (0, d_idx)),
            resident((2 * LANES, e)),
        ]

    state = pltpu.VMEM((bsz, D_STATE, e), F32)
    fwd = lambda j: j
    yf = pl.pallas_call(
        functools.partial(_ssd_fwd_kernel, n_sub=n_sub),
        grid=(nblk,),
        in_specs=scan_specs(fwd, 0),
        out_specs=pl.BlockSpec((bsz, ts, e), lambda j: (0, j, 0)),
        out_shape=jax.ShapeDtypeStruct((bsz, s, e), BF16),
        scratch_shapes=[state],
        compiler_params=params,
        name="ssd_scan_fwd",
    )(xbc, xbc, xbc, dt, alog_pad, e2)

    bwd = lambda j: nblk - 1 - j
    tile_b = lambda j: (0, bwd(j), 0)
    return pl.pallas_call(
        functools.partial(_ssd_bwd_out_kernel, n_sub=n_sub, final=final),
        grid=(nblk,),
        in_specs=scan_specs(bwd, 1) + [
            pl.BlockSpec((bsz, ts, e), tile_b),
            pl.BlockSpec((bsz, ts, e), tile_b),
            pl.BlockSpec((bsz, ts, d), tile_b),
            resident((1, e)),
            resident((1, e)),
            resident((e, d)),
            resident((1, d)),
        ],
        out_specs=pl.BlockSpec((bsz, ts, d), tile_b),
        out_shape=jax.ShapeDtypeStruct((bsz, s, d), F32),
        scratch_shapes=[state, pltpu.VMEM((bsz, CHUNK, e), F32)],
        compiler_params=params,
        name="ssd_scan_bwd_out",
    )(xbc, xbc, xbc, dt, alog_pad, e2, yf, z, h, dexp, gnw.reshape(1, e), w_out.astype(BF16), fnw.reshape(1, d))


def _pad_lanes(v):
    return jnp.pad(v, ((0, 0), (0, LANES - SSD_HEADS))).reshape(1, 2 * LANES)


def _expand_matrix():
    r = jnp.arange(2 * LANES)[:, None] % LANES
    c = jnp.arange(D_INNER)[None, :] // HEADDIM
    return (r == c).astype(BF16)


def _ssd_layer(h, nw, w_in, conv_w, conv_b, dt_bias, a_log, d_skip, gnw, w_out, fnw, *, final):
    d = h.shape[-1]
    e, cd = D_INNER, SSD_CONV_DIM
    w_zx = w_in[:, :e + cd].astype(BF16)
    w_dt = jnp.pad(w_in[:, e + cd:].reshape(d, 2, SSD_HEADS),
                   ((0, 0), (0, 0), (0, LANES - SSD_HEADS))).reshape(d, 2 * LANES).astype(BF16)
    z, xbc, dt = _ssd_in_layer(h, nw, w_zx, w_dt, conv_w, conv_b, _pad_lanes(dt_bias))
    dexp = jnp.repeat(d_skip, HEADDIM).reshape(1, e)
    return _ssd_scans(h, xbc, z, dt, _pad_lanes(a_log), _expand_matrix(), dexp, gnw, w_out, fnw, final=final)


def kernel(x, norm_w, final_norm_w, cm_w_in, cm_dw_w, cm_dw_b, cm_ln_w, cm_ln_b, cm_w_out, ssd_w_in,
           ssd_conv_w, ssd_conv_b, ssd_dt_bias, ssd_A_log, ssd_D, ssd_norm_w, ssd_w_out):
    depth = norm_w.shape[0]
    assert depth % 2 == 0
    h = x
    for i in range(depth):
        j = i // 2
        if i % 2 == 0:
            h = _conformer_layer(h, norm_w[i], cm_w_in[j], cm_dw_w[j], cm_dw_b[j], cm_ln_w[j],
                                 cm_ln_b[j], cm_w_out[j])
        else:
            h = _ssd_layer(h, norm_w[i], ssd_w_in[j], ssd_conv_w[j], ssd_conv_b[j], ssd_dt_bias[j],
                           ssd_A_log[j], ssd_D[j], ssd_norm_w[j], ssd_w_out[j], final_norm_w,
                           final=(i == depth - 1))
    return h
```
